```python
import jax, jax.numpy as jnp
from jax import lax
import numpy as np

D_MODEL = 2048
BATCH = 8
SEQ = 2048
DEPTH = 4

N_META = 16
N_CONV_LAYERS = max(DEPTH // 2, 1)
N_ATTN_LAYERS = max(DEPTH - N_CONV_LAYERS, 1)
CONV_WIDTH = 3
N_HEADS = 16
HEAD_DIM = D_MODEL // N_HEADS
D_FF = ((8 * D_MODEL // 3 + 255) // 256) * 256
Q_BLOCK = 128
DEEPNORM_ALPHA = (2 * DEPTH) ** 0.25
DEEPNORM_BETA = (8 * DEPTH) ** -0.25
LN_EPS = 1e-5

kernel_name = "yoco_shortconv_stickbreaking_macaron"


def layer_norm(x, g, b):
    xf = x.astype(jnp.float32)
    mu = jnp.mean(xf, axis=-1, keepdims=True)
    var = jnp.mean(jnp.square(xf - mu), axis=-1, keepdims=True)
    y = (xf - mu) * lax.rsqrt(var + LN_EPS)
    return (y * g.astype(jnp.float32) + b.astype(jnp.float32)).astype(x.dtype)


def swiglu(x, w_in, w_out):
    hid = x @ w_in
    gate = hid[..., :D_FF]
    up = hid[..., D_FF:]
    return (jax.nn.silu(gate) * up) @ w_out


def short_conv_mixer(x, w_in, conv_w, conv_b, w_out):
    T = x.shape[1]
    D = x.shape[2]
    proj = x @ w_in
    b_gate = proj[..., :D]
    c_gate = proj[..., D:2 * D]
    u = proj[..., 2 * D:]
    v = c_gate * u
    vp = jnp.pad(v, ((0, 0), (CONV_WIDTH - 1, 0), (0, 0)))
    conv = vp[:, 0:T] * conv_w[0]
    for tap in range(1, CONV_WIDTH):
        conv = conv + vp[:, tap:tap + T] * conv_w[tap]
    conv = conv + conv_b
    return (b_gate * conv) @ w_out


def split_heads(t):
    bsz, T = t.shape[0], t.shape[1]
    return t.reshape(bsz, T, N_HEADS, HEAD_DIM).transpose(0, 2, 1, 3)


def block_bounds(T):
    bounds = [0]
    if T > N_META:
        bounds.append(N_META)
        pos = N_META + Q_BLOCK
        while pos < T:
            bounds.append(pos)
            pos += Q_BLOCK
    bounds.append(T)
    out = []
    for s, e in zip(bounds[:-1], bounds[1:]):
        if e > s:
            out.append((int(s), int(e)))
    return out


def stick_breaking_attention(q, k, v):
    T = q.shape[2]
    scale = HEAD_DIM ** -0.5
    outs = []
    for s, e in block_bounds(T):
        qb = q[:, :, s:e]
        kb = k[:, :, :e]
        vb = v[:, :, :e]
        z = jnp.einsum('bhqd,bhkd->bhqk', qb, kb).astype(jnp.float32) * scale
        tq = jnp.arange(s, e, dtype=jnp.int32)[:, None]
        tk = jnp.arange(0, e, dtype=jnp.int32)[None, :]
        causal = tk < tq
        ls_neg = jax.nn.log_sigmoid(-z)
        log_keep = jnp.where(causal, ls_neg, 0.0)
        log_survive = lax.cumsum(log_keep, axis=3, reverse=True) - log_keep
        a = jnp.where(causal, jnp.exp(z + ls_neg + log_survive), 0.0)
        outs.append(jnp.einsum('bhqk,bhkd->bhqd', a.astype(vb.dtype), vb))
    return jnp.concatenate(outs, axis=2)


def setup_inputs(seed: int = 0) -> dict:
    key = jax.random.key(seed)
    ks = jax.random.split(key, 13)
    f32 = jnp.float32
    d, F = D_MODEL, D_FF
    return {
        "x": jax.random.normal(ks[0], (BATCH, SEQ, d), f32),
        "meta_tokens": jax.random.normal(ks[1], (N_META, d), f32),
        "ln_gain": 1.0 + 0.02 * jax.random.normal(ks[2], (DEPTH, 3, d), f32),
        "ln_bias": 0.02 * jax.random.normal(ks[3], (DEPTH, 3, d), f32),
        "ffn_w_in": jax.random.normal(ks[4], (DEPTH, 2, d, 2 * F), f32) * d ** -0.5,
        "ffn_w_out": jax.random.normal(ks[5], (DEPTH, 2, F, d), f32) * (F ** -0.5 * DEEPNORM_BETA),
        "conv_w_in": jax.random.normal(ks[6], (N_CONV_LAYERS, d, 3 * d), f32) * d ** -0.5,
        "conv_w": jax.random.normal(ks[7], (N_CONV_LAYERS, CONV_WIDTH, d), f32) * CONV_WIDTH ** -0.5,
        "conv_b": 0.02 * jax.random.normal(ks[8], (N_CONV_LAYERS, d), f32),
        "conv_w_out": jax.random.normal(ks[9], (N_CONV_LAYERS, d, d), f32) * (d ** -0.5 * DEEPNORM_BETA),
        "sb_w_q": jax.random.normal(ks[10], (N_ATTN_LAYERS, d, d), f32) * d ** -0.5,
        "sb_w_kv": jax.random.normal(ks[11], (d, 2 * d), f32) * d ** -0.5,
        "sb_w_o": jax.random.normal(ks[12], (N_ATTN_LAYERS, d, d), f32) * (d ** -0.5 * DEEPNORM_BETA),
    }


def reference(x, meta_tokens, ln_gain, ln_bias, ffn_w_in, ffn_w_out, conv_w_in, conv_w,
              conv_b, conv_w_out, sb_w_q, sb_w_kv, sb_w_o):
    bsz = x.shape[0]
    meta = jnp.broadcast_to(meta_tokens[None].astype(x.dtype), (bsz, N_META, x.shape[2]))
    h = jnp.concatenate([meta, x], axis=1)
    k_shared = None
    v_shared = None
    for i in range(DEPTH):
        h = layer_norm(DEEPNORM_ALPHA * h + 0.5 * swiglu(h, ffn_w_in[i, 0], ffn_w_out[i, 0]),
                       ln_gain[i, 0], ln_bias[i, 0])
        if i < N_CONV_LAYERS:
            mix = short_conv_mixer(h, conv_w_in[i], conv_w[i], conv_b[i], conv_w_out[i])
        else:
            j = i - N_CONV_LAYERS
            q = split_heads(h @ sb_w_q[j])
            o = stick_breaking_attention(q, k_shared, v_shared)
            o = o.transpose(0, 2, 1, 3).reshape(h.shape)
            mix = o @ sb_w_o[j]
        h = layer_norm(DEEPNORM_ALPHA * h + mix, ln_gain[i, 1], ln_bias[i, 1])
        h = layer_norm(DEEPNORM_ALPHA * h + 0.5 * swiglu(h, ffn_w_in[i, 1], ffn_w_out[i, 1]),
                       ln_gain[i, 2], ln_bias[i, 2])
        if i == N_CONV_LAYERS - 1:
            kv = h @ sb_w_kv
            k_shared = split_heads(kv[..., :D_MODEL])
            v_shared = split_heads(kv[..., D_MODEL:])
    return h[:, N_META:]
```

```python
import functools

import jax
import jax.numpy as jnp
from jax import lax
from jax.experimental import pallas as pl
from jax.experimental.pallas import tpu as pltpu

LN_EPS = 1e-5
HEAD_DIM = 128
LANES = 128
SUBLANES_BF16 = 16
HALO_ROWS = 8
VMEM_LIMIT_CAP = 60000 * 1024
COMPILER_SCRATCH_BYTES = 4 << 20

BF16 = jnp.bfloat16
F32 = jnp.float32


def _vmem_budget(*, pipelined, resident, temps):
    return 2 * pipelined + resident + temps + COMPILER_SCRATCH_BYTES


def _compiler_params(semantics, vmem_bytes):
    return pltpu.CompilerParams(
        dimension_semantics=semantics,
        vmem_limit_bytes=int(min(VMEM_LIMIT_CAP, vmem_bytes)),
    )


def _layer_norm(y, gain, bias):
    mu = jnp.mean(y, axis=-1, keepdims=True)
    yc = y - mu
    var = jnp.mean(yc * yc, axis=-1, keepdims=True)
    return yc * lax.rsqrt(var + LN_EPS) * gain + bias


def _dot(a, b):
    return jnp.dot(a, b, preferred_element_type=F32)


def _ffn_kernel(x_ref, wg_ref, wu_ref, wo_ref, g_ref, b_ref, o_ref, xb_ref, *, alpha, nf):
    f = pl.program_id(1)

    @pl.when(f == 0)
    def _():
        xb_ref[...] = x_ref[...].astype(BF16)
        o_ref[...] = jnp.zeros_like(o_ref)

    xb = xb_ref[...]
    gate = _dot(xb, wg_ref[...])
    up = _dot(xb, wu_ref[...])
    act = (gate * jax.nn.sigmoid(gate) * up).astype(BF16)
    o_ref[...] += _dot(act, wo_ref[...])

    @pl.when(f == nf - 1)
    def _():
        y = alpha * x_ref[...] + 0.5 * o_ref[...]
        o_ref[...] = _layer_norm(y, g_ref[...], b_ref[...])


def _ffn_ln(h, w_in, w_out, gain, bias, li, lj, lk, *, alpha, tm, tf):
    m, d = h.shape
    ff = w_out.shape[2]
    nf = ff // tf
    vmem = _vmem_budget(pipelined=2 * tm * d * 4 + 3 * d * tf * 2 + 2 * d * 4,
                        resident=tm * d * 2, temps=4 * tm * tf * 4 + 3 * tm * d * 4)
    return pl.pallas_call(
        functools.partial(_ffn_kernel, alpha=alpha, nf=nf),
        grid=(m // tm, nf),
        in_specs=[
            pl.BlockSpec((tm, d), lambda i, f: (i, 0)),
            pl.BlockSpec((None, None, d, tf), lambda i, f: (li, lj, 0, f)),
            pl.BlockSpec((None, None, d, tf), lambda i, f: (li, lj, 0, nf + f)),
            pl.BlockSpec((None, None, tf, d), lambda i, f: (li, lj, f, 0)),
            pl.BlockSpec((None, None, 1, d), lambda i, f: (li, lk, 0, 0)),
            pl.BlockSpec((None, None, 1, d), lambda i, f: (li, lk, 0, 0)),
        ],
        out_specs=pl.BlockSpec((tm, d), lambda i, f: (i, 0)),
        out_shape=jax.ShapeDtypeStruct((m, d), F32),
        scratch_shapes=[pltpu.VMEM((tm, d), BF16)],
        compiler_params=_compiler_params(("parallel", "arbitrary"), vmem),
        name="ffn_ln",
    )(h, w_in, w_in, w_out, gain, bias)


def _proj_kernel(x_ref, w_ref, o_ref, xb_ref):
    @pl.when(pl.program_id(1) == 0)
    def _():
        xb_ref[...] = x_ref[...].astype(BF16)

    o_ref[...] = _dot(xb_ref[...], w_ref[...]).astype(o_ref.dtype)


def _proj(h, w, li, *, tm, tn):
    m, d = h.shape
    n = w.shape[2]
    vmem = _vmem_budget(pipelined=tm * d * 4 + d * tn * 2 + tm * tn * 2,
                        resident=tm * d * 2, temps=2 * tm * tn * 4)
    return pl.pallas_call(
        _proj_kernel,
        grid=(m // tm, n // tn),
        in_specs=[
            pl.BlockSpec((tm, d), lambda i, j: (i, 0)),
            pl.BlockSpec((None, d, tn), lambda i, j: (li, 0, j)),
        ],
        out_specs=pl.BlockSpec((tm, tn), lambda i, j: (i, j)),
        out_shape=jax.ShapeDtypeStruct((m, n), BF16),
        scratch_shapes=[pltpu.VMEM((tm, d), BF16)],
        compiler_params=_compiler_params(("parallel", "arbitrary"), vmem),
        name="proj",
    )(h, w)


def _conv_in_kernel(x_ref, wb_ref, wc_ref, wu_ref, b_ref, v_ref, xb_ref):
    @pl.when(pl.program_id(1) == 0)
    def _():
        xb_ref[...] = x_ref[...].astype(BF16)

    xb = xb_ref[...]
    b_ref[...] = _dot(xb, wb_ref[...]).astype(BF16)
    v_ref[...] = (_dot(xb, wc_ref[...]) * _dot(xb, wu_ref[...])).astype(BF16)


def _conv_in(h, w_in, li, *, tm, tn):
    m, d = h.shape
    nd = d // tn
    vmem = _vmem_budget(pipelined=tm * d * 4 + 3 * d * tn * 2 + 2 * tm * tn * 2,
                        resident=tm * d * 2, temps=4 * tm * tn * 4)
    return pl.pallas_call(
        _conv_in_kernel,
        grid=(m // tm, nd),
        in_specs=[
            pl.BlockSpec((tm, d), lambda i, j: (i, 0)),
            pl.BlockSpec((None, d, tn), lambda i, j: (li, 0, j)),
            pl.BlockSpec((None, d, tn), lambda i, j: (li, 0, nd + j)),
            pl.BlockSpec((None, d, tn), lambda i, j: (li, 0, 2 * nd + j)),
        ],
        out_specs=[
            pl.BlockSpec((tm, tn), lambda i, j: (i, j)),
            pl.BlockSpec((tm, tn), lambda i, j: (i, j)),
        ],
        out_shape=[jax.ShapeDtypeStruct((m, d), BF16), jax.ShapeDtypeStruct((m, d), BF16)],
        scratch_shapes=[pltpu.VMEM((tm, d), BF16)],
        compiler_params=_compiler_params(("parallel", "arbitrary"), vmem),
        name="conv_in",
    )(h, w_in, w_in, w_in)


def _mix_out_kernel(lhs_ref, w_ref, x_ref, g_ref, b_ref, o_ref, *, alpha):
    y = alpha * x_ref[...] + _dot(lhs_ref[...], w_ref[...])
    o_ref[...] = _layer_norm(y, g_ref[...], b_ref[...])


def _conv_out_kernel(bg_ref, v_ref, halo_ref, cw_ref, cb_ref, w_ref, x_ref, g_ref, b_ref,
                     o_ref, *, alpha):
    v0 = v_ref[...].astype(F32)
    tm = v0.shape[0]
    halo = halo_ref[...].astype(F32)
    prev1 = halo[HALO_ROWS - 1:HALO_ROWS]
    prev2 = halo[HALO_ROWS - 2:HALO_ROWS - 1]
    row = lax.broadcasted_iota(jnp.int32, (tm, 1), 0)
    v1 = jnp.where(row == 0, prev1, pltpu.roll(v0, 1, axis=0))
    v2 = jnp.where(row == 0, prev2, jnp.where(row == 1, prev1, pltpu.roll(v0, 2, axis=0)))
    cw = cw_ref[...]
    conv = v2 * cw[0:1] + v1 * cw[1:2] + v0 * cw[2:3] + cb_ref[...]
    lhs = (bg_ref[...].astype(F32) * conv).astype(BF16)
    y = alpha * x_ref[...] + _dot(lhs, w_ref[...])
    o_ref[...] = _layer_norm(y, g_ref[...], b_ref[...])


def _mix_out_specs(tm, d, wi, li, lk):
    return [
        pl.BlockSpec((None, d, d), lambda i: (wi, 0, 0)),
        pl.BlockSpec((tm, d), lambda i: (i, 0)),
        pl.BlockSpec((None, None, 1, d), lambda i: (li, lk, 0, 0)),
        pl.BlockSpec((None, None, 1, d), lambda i: (li, lk, 0, 0)),
    ]


def _mix_out_ln(lhs, w, h, gain, bias, wi, li, lk, *, alpha, tm):
    m, d = h.shape
    vmem = _vmem_budget(pipelined=tm * d * 2 + d * d * 2 + 2 * tm * d * 4 + 2 * d * 4,
                        resident=0, temps=4 * tm * d * 4)
    return pl.pallas_call(
        functools.partial(_mix_out_kernel, alpha=alpha),
        grid=(m // tm,),
        in_specs=[pl.BlockSpec((tm, d), lambda i: (i, 0))] + _mix_out_specs(tm, d, wi, li, lk),
        out_specs=pl.BlockSpec((tm, d), lambda i: (i, 0)),
        out_shape=jax.ShapeDtypeStruct((m, d), F32),
        compiler_params=_compiler_params(("parallel",), vmem),
        name="mix_out_ln",
    )(lhs, w, h, gain, bias)


def _conv_out_ln(bg, v, halo, cw, cb, w, h, gain, bias, ci, li, lk, *, alpha, tm):
    m, d = h.shape
    taps = cw.shape[1]
    vmem = _vmem_budget(pipelined=2 * tm * d * 2 + d * d * 2 + 2 * tm * d * 4
                        + (HALO_ROWS + taps + 3) * d * 4, resident=0, temps=8 * tm * d * 4)
    return pl.pallas_call(
        functools.partial(_conv_out_kernel, alpha=alpha),
        grid=(m // tm,),
        in_specs=[
            pl.BlockSpec((tm, d), lambda i: (i, 0)),
            pl.BlockSpec((tm, d), lambda i: (i, 0)),
            pl.BlockSpec((None, HALO_ROWS, d), lambda i: (i, 0, 0)),
            pl.BlockSpec((None, taps, d), lambda i: (ci, 0, 0)),
            pl.BlockSpec((None, 1, d), lambda i: (ci, 0, 0)),
        ] + _mix_out_specs(tm, d, ci, li, lk),
        out_specs=pl.BlockSpec((tm, d), lambda i: (i, 0)),
        out_shape=jax.ShapeDtypeStruct((m, d), F32),
        compiler_params=_compiler_params(("parallel",), vmem),
        name="conv_out_ln",
    )(bg, v, halo, cw, cb, w, h, gain, bias)


def _strict_upper(n):
    r = lax.broadcasted_iota(jnp.int32, (n, n), 0)
    c = lax.broadcasted_iota(jnp.int32, (n, n), 1)
    return (r > c).astype(BF16)


def _sb_block(q, k_blk, v_blk, upper, carry, mask, scale):
    z = lax.dot_general(q, k_blk, (((1,), (1,)), ((), ())), preferred_element_type=F32) * scale
    softplus = jnp.maximum(z, 0.0) + jnp.log1p(jnp.exp(-jnp.abs(z)))
    log_keep = -softplus
    if mask is not None:
        log_keep = jnp.where(mask, log_keep, 0.0)
    hi = log_keep.astype(BF16)
    lo = (log_keep - hi.astype(F32)).astype(BF16)
    survive = _dot(hi, upper) + _dot(lo, upper)
    a = jnp.exp((z - softplus) + carry + survive)
    if mask is not None:
        a = jnp.where(mask, a, 0.0)
    return _dot(a.astype(BF16), v_blk), jnp.sum(log_keep, axis=-1, keepdims=True)


def _attn_kernel(q_ref, k_ref, v_ref, km_ref, vm_ref, o_ref, acc_ref, *, tq, n_meta, scale):
    qi = pl.program_id(2)
    q = q_ref[...]
    upper = _strict_upper(tq)
    row = lax.broadcasted_iota(jnp.int32, (tq, tq), 0)
    col = lax.broadcasted_iota(jnp.int32, (tq, tq), 1)

    start = pl.multiple_of(qi * tq, tq)
    pv, tot = _sb_block(q, k_ref[pl.ds(start, tq), :], v_ref[pl.ds(start, tq), :], upper,
                        jnp.zeros((tq, 1), F32), col < row, scale)
    acc_ref[...] = pv

    def body(i, carry):
        s = pl.multiple_of((qi - 1 - i) * tq, tq)
        pv, tot = _sb_block(q, k_ref[pl.ds(s, tq), :], v_ref[pl.ds(s, tq), :], upper,
                            carry, None, scale)
        acc_ref[...] += pv
        return carry + tot

    carry = lax.fori_loop(0, qi, body, tot)

    n_pad = km_ref.shape[0]
    mcol = lax.broadcasted_iota(jnp.int32, (tq, n_pad), 1)
    pv, _ = _sb_block(q, km_ref[...], vm_ref[...], _strict_upper(n_pad), carry,
                      mcol < n_meta, scale)
    o_ref[...] = (acc_ref[...] + pv).astype(o_ref.dtype)


def _attention(q, kv, kv_meta, *, batch, seq, n_meta, tq):
    m, d = q.shape
    heads = d // HEAD_DIM
    nq = seq // tq
    n_pad = kv_meta.shape[0]
    vmem = _vmem_budget(pipelined=2 * (tq + seq + n_pad) * HEAD_DIM * 2,
                        resident=tq * HEAD_DIM * 4, temps=16 * tq * tq * 4)
    return pl.pallas_call(
        functools.partial(_attn_kernel, tq=tq, n_meta=n_meta, scale=HEAD_DIM ** -0.5),
        grid=(batch, heads, nq),
        in_specs=[
            pl.BlockSpec((tq, HEAD_DIM), lambda b, h, i: (b * nq + i, h)),
            pl.BlockSpec((seq, HEAD_DIM), lambda b, h, i: (b, h)),
            pl.BlockSpec((seq, HEAD_DIM), lambda b, h, i: (b, heads + h)),
            pl.BlockSpec((n_pad, HEAD_DIM), lambda b, h, i: (0, h)),
            pl.BlockSpec((n_pad, HEAD_DIM), lambda b, h, i: (0, heads + h)),
        ],
        out_specs=pl.BlockSpec((tq, HEAD_DIM), lambda b, h, i: (b * nq + i, h)),
        out_shape=jax.ShapeDtypeStruct((m, d), BF16),
        scratch_shapes=[pltpu.VMEM((tq, HEAD_DIM), F32)],
        compiler_params=_compiler_params(("parallel", "parallel", "arbitrary"), vmem),
        name="sb_attention",
    )(q, kv, kv, kv_meta, kv_meta)


def _conv_halo(v_main, v_meta, *, batch, seq, tm):
    d = v_main.shape[1]
    nt = seq // tm
    tails = v_main.reshape(batch, nt, tm, d)[:, :nt - 1, tm - 2:, :]
    first = jnp.broadcast_to(v_meta[None, None, -2:, :], (batch, 1, 2, d))
    prev = jnp.concatenate([first, tails], axis=1).reshape(batch * nt, 2, d)
    return jnp.pad(prev, ((0, 0), (HALO_ROWS - 2, 0), (0, 0)))


def kernel(x, meta_tokens, ln_gain, ln_bias, ffn_w_in, ffn_w_out, conv_w_in, conv_w, conv_b,
           conv_w_out, sb_w_q, sb_w_kv, sb_w_o):
    batch, seq, d = x.shape
    n_meta = meta_tokens.shape[0]
    depth = ffn_w_in.shape[0]
    n_conv = conv_w_in.shape[0]
    alpha = float((2 * depth) ** 0.25)
    assert d % HEAD_DIM == 0 and n_meta % SUBLANES_BF16 == 0 and n_meta <= LANES

    tm = 512
    tf = 512
    tn = 512
    tq = 256

    w_in = ffn_w_in.astype(BF16)
    w_out = ffn_w_out.astype(BF16)
    cw_in = conv_w_in.astype(BF16)
    cw_out = conv_w_out.astype(BF16)
    w_q = sb_w_q.astype(BF16)
    w_kv = sb_w_kv.astype(BF16)[None]
    w_o = sb_w_o.astype(BF16)
    gain = ln_gain.reshape(depth, 3, 1, d)
    bias = ln_bias.reshape(depth, 3, 1, d)
    cb = conv_b.reshape(n_conv, 1, d)

    h = x.reshape(batch * seq, d)
    hm = meta_tokens.astype(x.dtype)
    kv = kv_meta = None
    for i in range(depth):
        meta_live = i < n_conv
        h = _ffn_ln(h, w_in, w_out, gain, bias, i, 0, 0, alpha=alpha, tm=tm, tf=tf)
        if meta_live:
            hm = _ffn_ln(hm, w_in, w_out, gain, bias, i, 0, 0, alpha=alpha, tm=n_meta, tf=tf)
        if i < n_conv:
            bg, v = _conv_in(h, cw_in, i, tm=tm, tn=tn)
            bgm, vm = _conv_in(hm, cw_in, i, tm=n_meta, tn=tn)
            halo = _conv_halo(v, vm, batch=batch, seq=seq, tm=tm)
            h = _conv_out_ln(bg, v, halo, conv_w, cb, cw_out, h, gain, bias, i, i, 1,
                             alpha=alpha, tm=tm)
            halo_m = jnp.zeros((1, HALO_ROWS, d), BF16)
            hm = _conv_out_ln(bgm, vm, halo_m, conv_w, cb, cw_out, hm, gain, bias, i, i, 1,
                              alpha=alpha, tm=n_meta)
        else:
            j = i - n_conv
            q = _proj(h, w_q, j, tm=tm, tn=tn)
            o = _attention(q, kv, kv_meta, batch=batch, seq=seq, n_meta=n_meta, tq=tq)
            h = _mix_out_ln(o, w_o, h, gain, bias, j, i, 1, alpha=alpha, tm=tm)
        h = _ffn_ln(h, w_in, w_out, gain, bias, i, 1, 2, alpha=alpha, tm=tm, tf=tf)
        if meta_live:
            hm = _ffn_ln(hm, w_in, w_out, gain, bias, i, 1, 2, alpha=alpha, tm=n_meta, tf=tf)
        if i == n_conv - 1:
            kv = _proj(h, w_kv, 0, tm=tm, tn=tn)
            kvm = _proj(hm, w_kv, 0, tm=n_meta, tn=tn)
            kv_meta = jnp.pad(kvm, ((0, LANES - n_meta), (0, 0)))
    return h.reshape(batch, seq, d)
```

```python
import functools

import jax
import jax.numpy as jnp
from jax import lax
from jax.experimental import pallas as pl
from jax.experimental.pallas import tpu as pltpu

LN_EPS = 1e-5
HEAD_DIM = 128
LANES = 128
SUBLANES_BF16 = 16
HALO_ROWS = 8
VMEM_LIMIT_CAP = 60000 * 1024
COMPILER_SCRATCH_BYTES = 4 << 20

LOG2E = 1.4426950408889634

BF16 = jnp.bfloat16
F32 = jnp.float32


def _vmem_budget(*, pipelined, resident, temps):
    return 2 * pipelined + resident + temps + COMPILER_SCRATCH_BYTES


def _compiler_params(semantics, vmem_bytes):
    return pltpu.CompilerParams(
        dimension_semantics=semantics,
        vmem_limit_bytes=int(min(VMEM_LIMIT_CAP, vmem_bytes)),
    )


def _layer_norm(y, gain, bias):
    mu = jnp.mean(y, axis=-1, keepdims=True)
    yc = y - mu
    var = jnp.mean(yc * yc, axis=-1, keepdims=True)
    return yc * lax.rsqrt(var + LN_EPS) * gain + bias


def _dot(a, b):
    return jnp.dot(a, b, preferred_element_type=F32)


def _ffn_kernel(x_ref, wg_ref, wu_ref, wo_ref, g_ref, b_ref, o_ref, xb_ref, *, alpha, nf):
    f = pl.program_id(1)

    @pl.when(f == 0)
    def _():
        xb_ref[...] = x_ref[...].astype(BF16)
        o_ref[...] = jnp.zeros_like(o_ref)

    xb = xb_ref[...]
    gate = _dot(xb, wg_ref[...])
    up = _dot(xb, wu_ref[...])
    act = (gate * jax.nn.sigmoid(gate) * up).astype(BF16)
    o_ref[...] += _dot(act, wo_ref[...])

    @pl.when(f == nf - 1)
    def _():
        y = alpha * x_ref[...] + 0.5 * o_ref[...]
        o_ref[...] = _layer_norm(y, g_ref[...], b_ref[...])


def _ffn_ln(h, w_in, w_out, gain, bias, li, lj, lk, *, alpha, tm, tf):
    m, d = h.shape
    ff = w_out.shape[2]
    nf = ff // tf
    vmem = _vmem_budget(pipelined=2 * tm * d * 4 + 3 * d * tf * 2 + 2 * d * 4,
                        resident=tm * d * 2, temps=4 * tm * tf * 4 + 3 * tm * d * 4)
    return pl.pallas_call(
        functools.partial(_ffn_kernel, alpha=alpha, nf=nf),
        grid=(m // tm, nf),
        in_specs=[
            pl.BlockSpec((tm, d), lambda i, f: (i, 0)),
            pl.BlockSpec((None, None, d, tf), lambda i, f: (li, lj, 0, f)),
            pl.BlockSpec((None, None, d, tf), lambda i, f: (li, lj, 0, nf + f)),
            pl.BlockSpec((None, None, tf, d), lambda i, f: (li, lj, f, 0)),
            pl.BlockSpec((None, None, 1, d), lambda i, f: (li, lk, 0, 0)),
            pl.BlockSpec((None, None, 1, d), lambda i, f: (li, lk, 0, 0)),
        ],
        out_specs=pl.BlockSpec((tm, d), lambda i, f: (i, 0)),
        out_shape=jax.ShapeDtypeStruct((m, d), F32),
        scratch_shapes=[pltpu.VMEM((tm, d), BF16)],
        compiler_params=_compiler_params(("parallel", "arbitrary"), vmem),
        name="ffn_ln",
    )(h, w_in, w_in, w_out, gain, bias)


def _proj_kernel(x_ref, w_ref, o_ref, xb_ref):
    @pl.when(pl.program_id(1) == 0)
    def _():
        xb_ref[...] = x_ref[...].astype(BF16)

    o_ref[...] = _dot(xb_ref[...], w_ref[...]).astype(o_ref.dtype)


def _proj(h, w, li, *, tm, tn):
    m, d = h.shape
    n = w.shape[2]
    vmem = _vmem_budget(pipelined=tm * d * 4 + d * tn * 2 + tm * tn * 2,
                        resident=tm * d * 2, temps=2 * tm * tn * 4)
    return pl.pallas_call(
        _proj_kernel,
        grid=(m // tm, n // tn),
        in_specs=[
            pl.BlockSpec((tm, d), lambda i, j: (i, 0)),
            pl.BlockSpec((None, d, tn), lambda i, j: (li, 0, j)),
        ],
        out_specs=pl.BlockSpec((tm, tn), lambda i, j: (i, j)),
        out_shape=jax.ShapeDtypeStruct((m, n), BF16),
        scratch_shapes=[pltpu.VMEM((tm, d), BF16)],
        compiler_params=_compiler_params(("parallel", "arbitrary"), vmem),
        name="proj",
    )(h, w)


def _conv_in_kernel(x_ref, wb_ref, wc_ref, wu_ref, b_ref, v_ref, xb_ref):
    @pl.when(pl.program_id(1) == 0)
    def _():
        xb_ref[...] = x_ref[...].astype(BF16)

    xb = xb_ref[...]
    b_ref[...] = _dot(xb, wb_ref[...]).astype(BF16)
    v_ref[...] = (_dot(xb, wc_ref[...]) * _dot(xb, wu_ref[...])).astype(BF16)


def _conv_in(h, w_in, li, *, tm, tn):
    m, d = h.shape
    nd = d // tn
    vmem = _vmem_budget(pipelined=tm * d * 4 + 3 * d * tn * 2 + 2 * tm * tn * 2,
                        resident=tm * d * 2, temps=4 * tm * tn * 4)
    return pl.pallas_call(
        _conv_in_kernel,
        grid=(m // tm, nd),
        in_specs=[
            pl.BlockSpec((tm, d), lambda i, j: (i, 0)),
            pl.BlockSpec((None, d, tn), lambda i, j: (li, 0, j)),
            pl.BlockSpec((None, d, tn), lambda i, j: (li, 0, nd + j)),
            pl.BlockSpec((None, d, tn), lambda i, j: (li, 0, 2 * nd + j)),
        ],
        out_specs=[
            pl.BlockSpec((tm, tn), lambda i, j: (i, j)),
            pl.BlockSpec((tm, tn), lambda i, j: (i, j)),
        ],
        out_shape=[jax.ShapeDtypeStruct((m, d), BF16), jax.ShapeDtypeStruct((m, d), BF16)],
        scratch_shapes=[pltpu.VMEM((tm, d), BF16)],
        compiler_params=_compiler_params(("parallel", "arbitrary"), vmem),
        name="conv_in",
    )(h, w_in, w_in, w_in)


def _mix_out_kernel(lhs_ref, w_ref, x_ref, g_ref, b_ref, o_ref, *, alpha):
    y = alpha * x_ref[...] + _dot(lhs_ref[...], w_ref[...])
    o_ref[...] = _layer_norm(y, g_ref[...], b_ref[...])


def _conv_out_kernel(bg_ref, v_ref, halo_ref, cw_ref, cb_ref, w_ref, x_ref, g_ref, b_ref,
                     o_ref, *, alpha):
    v0 = v_ref[...].astype(F32)
    tm = v0.shape[0]
    halo = halo_ref[...].astype(F32)
    prev1 = halo[HALO_ROWS - 1:HALO_ROWS]
    prev2 = halo[HALO_ROWS - 2:HALO_ROWS - 1]
    row = lax.broadcasted_iota(jnp.int32, (tm, 1), 0)
    v1 = jnp.where(row == 0, prev1, pltpu.roll(v0, 1, axis=0))
    v2 = jnp.where(row == 0, prev2, jnp.where(row == 1, prev1, pltpu.roll(v0, 2, axis=0)))
    cw = cw_ref[...]
    conv = v2 * cw[0:1] + v1 * cw[1:2] + v0 * cw[2:3] + cb_ref[...]
    lhs = (bg_ref[...].astype(F32) * conv).astype(BF16)
    y = alpha * x_ref[...] + _dot(lhs, w_ref[...])
    o_ref[...] = _layer_norm(y, g_ref[...], b_ref[...])


def _mix_out_specs(tm, d, wi, li, lk):
    return [
        pl.BlockSpec((None, d, d), lambda i: (wi, 0, 0)),
        pl.BlockSpec((tm, d), lambda i: (i, 0)),
        pl.BlockSpec((None, None, 1, d), lambda i: (li, lk, 0, 0)),
        pl.BlockSpec((None, None, 1, d), lambda i: (li, lk, 0, 0)),
    ]


def _mix_out_ln(lhs, w, h, gain, bias, wi, li, lk, *, alpha, tm):
    m, d = h.shape
    vmem = _vmem_budget(pipelined=tm * d * 2 + d * d * 2 + 2 * tm * d * 4 + 2 * d * 4,
                        resident=0, temps=4 * tm * d * 4)
    return pl.pallas_call(
        functools.partial(_mix_out_kernel, alpha=alpha),
        grid=(m // tm,),
        in_specs=[pl.BlockSpec((tm, d), lambda i: (i, 0))] + _mix_out_specs(tm, d, wi, li, lk),
        out_specs=pl.BlockSpec((tm, d), lambda i: (i, 0)),
        out_shape=jax.ShapeDtypeStruct((m, d), F32),
        compiler_params=_compiler_params(("parallel",), vmem),
        name="mix_out_ln",
    )(lhs, w, h, gain, bias)


def _conv_out_ln(bg, v, halo, cw, cb, w, h, gain, bias, ci, li, lk, *, alpha, tm):
    m, d = h.shape
    taps = cw.shape[1]
    vmem = _vmem_budget(pipelined=2 * tm * d * 2 + d * d * 2 + 2 * tm * d * 4
                        + (HALO_ROWS + taps + 3) * d * 4, resident=0, temps=8 * tm * d * 4)
    return pl.pallas_call(
        functools.partial(_conv_out_kernel, alpha=alpha),
        grid=(m // tm,),
        in_specs=[
            pl.BlockSpec((tm, d), lambda i: (i, 0)),
            pl.BlockSpec((tm, d), lambda i: (i, 0)),
            pl.BlockSpec((None, HALO_ROWS, d), lambda i: (i, 0, 0)),
            pl.BlockSpec((None, taps, d), lambda i: (ci, 0, 0)),
            pl.BlockSpec((None, 1, d), lambda i: (ci, 0, 0)),
        ] + _mix_out_specs(tm, d, ci, li, lk),
        out_specs=pl.BlockSpec((tm, d), lambda i: (i, 0)),
        out_shape=jax.ShapeDtypeStruct((m, d), F32),
        compiler_params=_compiler_params(("parallel",), vmem),
        name="conv_out_ln",
    )(bg, v, halo, cw, cb, w, h, gain, bias)


def _suffix_sum_matrix(n):
    r = lax.broadcasted_iota(jnp.int32, (2 * n, n), 0)
    c = lax.broadcasted_iota(jnp.int32, (2 * n, n), 1)
    return (jnp.where(r >= n, r - n, r) > c).astype(BF16)


def _sb_partials(q_ref, lanes, k_blks, suffix, mask, scale):
    raws = [lax.dot_general(q_ref[:, ln], k, (((1,), (1,)), ((), ())), preferred_element_type=F32)
            for ln, k in zip(lanes, k_blks)]
    log_betas, log_keeps = [], []
    for raw in raws:
        z = raw * scale
        soft = jnp.log(1.0 + jnp.exp2(jnp.abs(raw) * (-scale * LOG2E)))
        log_beta = jnp.minimum(z, 0.0) - soft
        log_keep = log_beta - z
        if mask is not None:
            log_keep = jnp.where(mask, log_keep, 0.0)
        log_betas.append(log_beta)
        log_keeps.append(log_keep)
    his = [lk.astype(BF16) for lk in log_keeps]
    los = [(lk - hi.astype(F32)).astype(BF16) for lk, hi in zip(log_keeps, his)]
    survives = [_dot(jnp.concatenate([hi, lo], axis=1), suffix[...]) for hi, lo in zip(his, los)]
    partials = [lb + sv for lb, sv in zip(log_betas, survives)]
    totals = [jnp.sum(lk, axis=-1, keepdims=True) for lk in log_keeps]
    return partials, totals


def _sb_weights(partials, carries, mask):
    out = []
    for partial, carry in zip(partials, carries):
        a = jnp.exp(partial + carry)
        if mask is not None:
            a = jnp.where(mask, a, 0.0)
        out.append(a.astype(BF16))
    return out


def _attn_kernel(q_ref, k_ref, v_ref, km_ref, vm_ref, o_ref, acc_ref, suffix_ref, *, tq, hp,
                 n_meta, scale):
    qi = pl.program_id(2)
    suffix_ref[...] = _suffix_sum_matrix(tq)
    row = lax.broadcasted_iota(jnp.int32, (tq, tq), 0)
    col = lax.broadcasted_iota(jnp.int32, (tq, tq), 1)
    causal = col < row
    n_pad = km_ref.shape[0]
    meta_valid = lax.broadcasted_iota(jnp.int32, (tq, n_pad), 1) < n_meta
    lanes = [slice(g * HEAD_DIM, (g + 1) * HEAD_DIM) for g in range(hp)]

    def kv_blocks(ref, start):
        return [ref[pl.ds(start, tq), ln] for ln in lanes]

    start = pl.multiple_of(qi * tq, tq)
    partials, carries = _sb_partials(q_ref, lanes, kv_blocks(k_ref, start), suffix_ref, causal,
                                     scale)
    meta_partials, _ = _sb_partials(q_ref, lanes, [km_ref[:, ln] for ln in lanes],
                                    _suffix_sum_matrix(n_pad), meta_valid, scale)
    weights = _sb_weights(partials, [0.0] * hp, causal)
    for ln, a, v in zip(lanes, weights, kv_blocks(v_ref, start)):
        acc_ref[:, ln] = _dot(a, v)

    def body(i, carries):
        s = pl.multiple_of((qi - 1 - i) * tq, tq)
        partials, totals = _sb_partials(q_ref, lanes, kv_blocks(k_ref, s), suffix_ref, None, scale)
        weights = _sb_weights(partials, carries, None)
        for ln, a, v in zip(lanes, weights, kv_blocks(v_ref, s)):
            acc_ref[:, ln] += _dot(a, v)
        return tuple(c + t for c, t in zip(carries, totals))

    carries = lax.fori_loop(0, qi, body, tuple(carries))

    weights = _sb_weights(meta_partials, carries, meta_valid)
    for ln, a in zip(lanes, weights):
        o_ref[:, ln] = (acc_ref[:, ln] + _dot(a, vm_ref[:, ln])).astype(o_ref.dtype)


def _attention(q, kv, kv_meta, *, batch, seq, n_meta, tq, hp):
    m, d = q.shape
    width = hp * HEAD_DIM
    groups = d // width
    nq = seq // tq
    n_pad = kv_meta.shape[0]
    vmem = _vmem_budget(pipelined=2 * (tq + seq + n_pad) * width * 2,
                        resident=tq * width * 4 + 2 * tq * tq * 2, temps=hp * 16 * tq * tq * 4)
    return pl.pallas_call(
        functools.partial(_attn_kernel, tq=tq, hp=hp, n_meta=n_meta, scale=HEAD_DIM ** -0.5),
        grid=(batch, groups, nq),
        in_specs=[
            pl.BlockSpec((tq, width), lambda b, h, i: (b * nq + i, h)),
            pl.BlockSpec((seq, width), lambda b, h, i: (b, h)),
            pl.BlockSpec((seq, width), lambda b, h, i: (b, groups + h)),
            pl.BlockSpec((n_pad, width), lambda b, h, i: (0, h)),
            pl.BlockSpec((n_pad, width), lambda b, h, i: (0, groups + h)),
        ],
        out_specs=pl.BlockSpec((tq, width), lambda b, h, i: (b * nq + i, h)),
        out_shape=jax.ShapeDtypeStruct((m, d), BF16),
        scratch_shapes=[pltpu.VMEM((tq, width), F32), pltpu.VMEM((2 * tq, tq), BF16)],
        compiler_params=_compiler_params(("parallel", "parallel", "arbitrary"), vmem),
        name="sb_attention",
    )(q, kv, kv, kv_meta, kv_meta)


def _conv_halo(v_main, v_meta, *, batch, seq, tm):
    d = v_main.shape[1]
    nt = seq // tm
    tails = v_main.reshape(batch, nt, tm, d)[:, :nt - 1, tm - 2:, :]
    first = jnp.broadcast_to(v_meta[None, None, -2:, :], (batch, 1, 2, d))
    prev = jnp.concatenate([first, tails], axis=1).reshape(batch * nt, 2, d)
    return jnp.pad(prev, ((0, 0), (HALO_ROWS - 2, 0), (0, 0)))


def kernel(x, meta_tokens, ln_gain, ln_bias, ffn_w_in, ffn_w_out, conv_w_in, conv_w, conv_b,
           conv_w_out, sb_w_q, sb_w_kv, sb_w_o):
    batch, seq, d = x.shape
    n_meta = meta_tokens.shape[0]
    depth = ffn_w_in.shape[0]
    n_conv = conv_w_in.shape[0]
    alpha = float((2 * depth) ** 0.25)
    assert d % HEAD_DIM == 0 and n_meta % SUBLANES_BF16 == 0 and n_meta <= LANES

    tm = 512
    tf = 512
    tn = 512
    tq = 256
    hp = 8

    w_in = ffn_w_in.astype(BF16)
    w_out = ffn_w_out.astype(BF16)
    cw_in = conv_w_in.astype(BF16)
    cw_out = conv_w_out.astype(BF16)
    w_q = sb_w_q.astype(BF16)
    w_kv = sb_w_kv.astype(BF16)[None]
    w_o = sb_w_o.astype(BF16)
    gain = ln_gain.reshape(depth, 3, 1, d)
    bias = ln_bias.reshape(depth, 3, 1, d)
    cb = conv_b.reshape(n_conv, 1, d)

    h = x.reshape(batch * seq, d)
    hm = meta_tokens.astype(x.dtype)
    kv = kv_meta = None
    for i in range(depth):
        meta_live = i < n_conv
        h = _ffn_ln(h, w_in, w_out, gain, bias, i, 0, 0, alpha=alpha, tm=tm, tf=tf)
        if meta_live:
            hm = _ffn_ln(hm, w_in, w_out, gain, bias, i, 0, 0, alpha=alpha, tm=n_meta, tf=tf)
        if i < n_conv:
            bg, v = _conv_in(h, cw_in, i, tm=tm, tn=tn)
            bgm, vm = _conv_in(hm, cw_in, i, tm=n_meta, tn=tn)
            halo = _conv_halo(v, vm, batch=batch, seq=seq, tm=tm)
            h = _conv_out_ln(bg, v, halo, conv_w, cb, cw_out, h, gain, bias, i, i, 1,
                             alpha=alpha, tm=tm)
            halo_m = jnp.zeros((1, HALO_ROWS, d), BF16)
            hm = _conv_out_ln(bgm, vm, halo_m, conv_w, cb, cw_out, hm, gain, bias, i, i, 1,
                              alpha=alpha, tm=n_meta)
        else:
            j = i - n_conv
            q = _proj(h, w_q, j, tm=tm, tn=tn)
            o = _attention(q, kv, kv_meta, batch=batch, seq=seq, n_meta=n_meta, tq=tq, hp=hp)
            h = _mix_out_ln(o, w_o, h, gain, bias, j, i, 1, alpha=alpha, tm=tm)
        h = _ffn_ln(h, w_in, w_out, gain, bias, i, 1, 2, alpha=alpha, tm=tm, tf=tf)
        if meta_live:
            hm = _ffn_ln(hm, w_in, w_out, gain, bias, i, 1, 2, alpha=alpha, tm=n_meta, tf=tf)
        if i == n_conv - 1:
            kv = _proj(h, w_kv, 0, tm=tm, tn=tn)
            kvm = _proj(hm, w_kv, 0, tm=n_meta, tn=tn)
            kv_meta = jnp.pad(kvm, ((0, LANES - n_meta), (0, 0)))
    return h.reshape(batch, seq, d)
```

```python
import functools

import jax
import jax.numpy as jnp
from jax import lax
from jax.experimental import pallas as pl
from jax.experimental.pallas import tpu as pltpu

LN_EPS = 1e-5
HEAD_DIM = 128
LANES = 128
SUBLANES_BF16 = 16
HALO_ROWS = 8
MXU_ROWS_PER_WEIGHT_TILE = 256
VMEM_LIMIT_CAP = 60000 * 1024
COMPILER_SCRATCH_BYTES = 4 << 20

LOG2E = 1.4426950408889634

BF16 = jnp.bfloat16
F32 = jnp.float32


def _vmem_budget(*, pipelined, resident, temps):
    return 2 * pipelined + resident + temps + COMPILER_SCRATCH_BYTES


def _compiler_params(semantics, vmem_bytes):
    return pltpu.CompilerParams(
        dimension_semantics=semantics,
        vmem_limit_bytes=int(min(VMEM_LIMIT_CAP, vmem_bytes)),
    )


def _layer_norm(y, gain, bias):
    mu = jnp.mean(y, axis=-1, keepdims=True)
    yc = y - mu
    var = jnp.mean(yc * yc, axis=-1, keepdims=True)
    return yc * lax.rsqrt(var + LN_EPS) * gain + bias


def _dot(a, b):
    return jnp.dot(a, b, preferred_element_type=F32)


def _store_ln(y, g_ref, b_ref, o_ref, ob_ref):
    out = _layer_norm(y, g_ref[...], b_ref[...])
    o_ref[...] = out
    ob_ref[...] = out.astype(BF16)


def _ffn_kernel(x_ref, xb_ref, wg_ref, wu_ref, wo_ref, g_ref, b_ref, o_ref, ob_ref, *, alpha, nf):
    f = pl.program_id(1)

    @pl.when(f == 0)
    def _():
        o_ref[...] = (2.0 * alpha) * x_ref[...]

    xb = xb_ref[...]
    gate = _dot(xb, wg_ref[...])
    up = _dot(xb, wu_ref[...])
    act = (gate * jax.nn.sigmoid(gate) * up).astype(BF16)
    o_ref[...] += _dot(act, wo_ref[...])

    @pl.when(f == nf - 1)
    def _():
        _store_ln(0.5 * o_ref[...], g_ref, b_ref, o_ref, ob_ref)


def _ffn_ln(h, hb, w_in, w_out, gain, bias, li, lj, lk, *, alpha, tm, tf):
    m, d = h.shape
    ff = w_out.shape[2]
    nf = ff // tf
    vmem = _vmem_budget(pipelined=2 * tm * d * 4 + 2 * tm * d * 2 + 3 * d * tf * 2 + 2 * d * 4,
                        resident=0, temps=4 * tm * tf * 4 + 3 * tm * d * 4)
    return pl.pallas_call(
        functools.partial(_ffn_kernel, alpha=alpha, nf=nf),
        grid=(m // tm, nf),
        in_specs=[
            pl.BlockSpec((tm, d), lambda i, f: (i, 0)),
            pl.BlockSpec((tm, d), lambda i, f: (i, 0)),
            pl.BlockSpec((None, None, d, tf), lambda i, f: (li, lj, 0, f)),
            pl.BlockSpec((None, None, d, tf), lambda i, f: (li, lj, 0, nf + f)),
            pl.BlockSpec((None, None, tf, d), lambda i, f: (li, lj, f, 0)),
            pl.BlockSpec((None, None, 1, d), lambda i, f: (li, lk, 0, 0)),
            pl.BlockSpec((None, None, 1, d), lambda i, f: (li, lk, 0, 0)),
        ],
        out_specs=[pl.BlockSpec((tm, d), lambda i, f: (i, 0))] * 2,
        out_shape=[jax.ShapeDtypeStruct((m, d), F32), jax.ShapeDtypeStruct((m, d), BF16)],
        compiler_params=_compiler_params(("parallel", "arbitrary"), vmem),
        name="ffn_ln",
    )(h, hb, w_in, w_in, w_out, gain, bias)


def _proj_kernel(xb_ref, w_ref, o_ref):
    o_ref[...] = _dot(xb_ref[...], w_ref[...]).astype(o_ref.dtype)


def _proj(hb, w, li, *, tm, tn):
    m, d = hb.shape
    n = w.shape[2]
    vmem = _vmem_budget(pipelined=tm * d * 2 + d * tn * 2 + tm * tn * 2,
                        resident=0, temps=2 * tm * tn * 4)
    return pl.pallas_call(
        _proj_kernel,
        grid=(m // tm, n // tn),
        in_specs=[
            pl.BlockSpec((tm, d), lambda i, j: (i, 0)),
            pl.BlockSpec((None, d, tn), lambda i, j: (li, 0, j)),
        ],
        out_specs=pl.BlockSpec((tm, tn), lambda i, j: (i, j)),
        out_shape=jax.ShapeDtypeStruct((m, n), BF16),
        compiler_params=_compiler_params(("parallel", "arbitrary"), vmem),
        name="proj",
    )(hb, w)


def _conv_in_kernel(xb_ref, wb_ref, wc_ref, wu_ref, b_ref, v_ref):
    xb = xb_ref[...]
    b_ref[...] = _dot(xb, wb_ref[...]).astype(BF16)
    v_ref[...] = (_dot(xb, wc_ref[...]) * _dot(xb, wu_ref[...])).astype(BF16)


def _conv_in(hb, w_in, li, *, tm, tn):
    m, d = hb.shape
    nd = d // tn
    vmem = _vmem_budget(pipelined=tm * d * 2 + 3 * d * tn * 2 + 2 * tm * tn * 2,
                        resident=0, temps=4 * tm * tn * 4)
    return pl.pallas_call(
        _conv_in_kernel,
        grid=(m // tm, nd),
        in_specs=[
            pl.BlockSpec((tm, d), lambda i, j: (i, 0)),
            pl.BlockSpec((None, d, tn), lambda i, j: (li, 0, j)),
            pl.BlockSpec((None, d, tn), lambda i, j: (li, 0, nd + j)),
            pl.BlockSpec((None, d, tn), lambda i, j: (li, 0, 2 * nd + j)),
        ],
        out_specs=[
            pl.BlockSpec((tm, tn), lambda i, j: (i, j)),
            pl.BlockSpec((tm, tn), lambda i, j: (i, j)),
        ],
        out_shape=[jax.ShapeDtypeStruct((m, d), BF16), jax.ShapeDtypeStruct((m, d), BF16)],
        compiler_params=_compiler_params(("parallel", "arbitrary"), vmem),
        name="conv_in",
    )(hb, w_in, w_in, w_in)


def _row_chunks(tm):
    rc = min(tm, MXU_ROWS_PER_WEIGHT_TILE)
    return [slice(r, r + rc) for r in range(0, tm, rc)]


def _residual_ln_chunks(chunks, products, x_ref, g_ref, b_ref, o_ref, ob_ref, alpha):
    for rows, prod in zip(chunks, products):
        out = _layer_norm(alpha * x_ref[rows, :] + prod, g_ref[...], b_ref[...])
        o_ref[rows, :] = out
        ob_ref[rows, :] = out.astype(BF16)


def _mix_out_kernel(lhs_ref, w_ref, x_ref, g_ref, b_ref, o_ref, ob_ref, *, alpha):
    chunks = _row_chunks(x_ref.shape[0])
    products = [_dot(lhs_ref[rows, :], w_ref[...]) for rows in chunks]
    _residual_ln_chunks(chunks, products, x_ref, g_ref, b_ref, o_ref, ob_ref, alpha)


def _conv_out_kernel(bg_ref, v_ref, halo_ref, cw_ref, cb_ref, w_ref, x_ref, g_ref, b_ref,
                     o_ref, ob_ref, *, alpha):
    chunks = _row_chunks(x_ref.shape[0])
    cw = cw_ref[...]
    products = []
    for rows in chunks:
        v0 = v_ref[rows, :].astype(F32)
        if rows.start == 0:
            before = halo_ref[...].astype(F32)
        else:
            before = v_ref[rows.start - SUBLANES_BF16:rows.start, :].astype(F32)
        prev1 = before[-1:]
        prev2 = before[-2:-1]
        row = lax.broadcasted_iota(jnp.int32, (v0.shape[0], 1), 0)
        v1 = jnp.where(row == 0, prev1, pltpu.roll(v0, 1, axis=0))
        v2 = jnp.where(row == 0, prev2, jnp.where(row == 1, prev1, pltpu.roll(v0, 2, axis=0)))
        conv = v2 * cw[0:1] + v1 * cw[1:2] + v0 * cw[2:3] + cb_ref[...]
        lhs = (bg_ref[rows, :].astype(F32) * conv).astype(BF16)
        products.append(_dot(lhs, w_ref[...]))
    _residual_ln_chunks(chunks, products, x_ref, g_ref, b_ref, o_ref, ob_ref, alpha)


def _ln_out(m, d, tm):
    return dict(
        out_specs=[pl.BlockSpec((tm, d), lambda i: (i, 0))] * 2,
        out_shape=[jax.ShapeDtypeStruct((m, d), F32), jax.ShapeDtypeStruct((m, d), BF16)],
    )


def _mix_out_specs(tm, d, wi, li, lk):
    return [
        pl.BlockSpec((None, d, d), lambda i: (wi, 0, 0)),
        pl.BlockSpec((tm, d), lambda i: (i, 0)),
        pl.BlockSpec((None, None, 1, d), lambda i: (li, lk, 0, 0)),
        pl.BlockSpec((None, None, 1, d), lambda i: (li, lk, 0, 0)),
    ]


def _mix_out_ln(lhs, w, h, gain, bias, wi, li, lk, *, alpha, tm):
    m, d = h.shape
    vmem = _vmem_budget(pipelined=2 * tm * d * 2 + d * d * 2 + 2 * tm * d * 4 + 2 * d * 4,
                        resident=0, temps=4 * tm * d * 4)
    return pl.pallas_call(
        functools.partial(_mix_out_kernel, alpha=alpha),
        grid=(m // tm,),
        in_specs=[pl.BlockSpec((tm, d), lambda i: (i, 0))] + _mix_out_specs(tm, d, wi, li, lk),
        **_ln_out(m, d, tm),
        compiler_params=_compiler_params(("parallel",), vmem),
        name="mix_out_ln",
    )(lhs, w, h, gain, bias)


def _conv_out_ln(bg, v, halo, cw, cb, w, h, gain, bias, ci, li, lk, *, alpha, tm):
    m, d = h.shape
    taps = cw.shape[1]
    vmem = _vmem_budget(pipelined=3 * tm * d * 2 + d * d * 2 + 2 * tm * d * 4
                        + (HALO_ROWS + taps + 3) * d * 4, resident=0, temps=8 * tm * d * 4)
    return pl.pallas_call(
        functools.partial(_conv_out_kernel, alpha=alpha),
        grid=(m // tm,),
        in_specs=[
            pl.BlockSpec((tm, d), lambda i: (i, 0)),
            pl.BlockSpec((tm, d), lambda i: (i, 0)),
            pl.BlockSpec((None, HALO_ROWS, d), lambda i: (i, 0, 0)),
            pl.BlockSpec((None, taps, d), lambda i: (ci, 0, 0)),
            pl.BlockSpec((None, 1, d), lambda i: (ci, 0, 0)),
        ] + _mix_out_specs(tm, d, ci, li, lk),
        **_ln_out(m, d, tm),
        compiler_params=_compiler_params(("parallel",), vmem),
        name="conv_out_ln",
    )(bg, v, halo, cw, cb, w, h, gain, bias)


def _suffix_sum_matrix(n):
    r = lax.broadcasted_iota(jnp.int32, (2 * n, n), 0)
    c = lax.broadcasted_iota(jnp.int32, (2 * n, n), 1)
    return (jnp.where(r >= n, r - n, r) > c).astype(BF16)


def _sb_partials(q_ref, lanes, k_blks, suffix, mask, scale):
    raws = [lax.dot_general(q_ref[:, ln], k, (((1,), (1,)), ((), ())), preferred_element_type=F32)
            for ln, k in zip(lanes, k_blks)]
    log_betas, log_keeps = [], []
    for raw in raws:
        z = raw * scale
        soft = jnp.log(1.0 + jnp.exp2(jnp.abs(raw) * (-scale * LOG2E)))
        log_beta = jnp.minimum(z, 0.0) - soft
        log_keep = log_beta - z
        if mask is not None:
            log_keep = jnp.where(mask, log_keep, 0.0)
        log_betas.append(log_beta)
        log_keeps.append(log_keep)
    his = [lk.astype(BF16) for lk in log_keeps]
    los = [(lk - hi.astype(F32)).astype(BF16) for lk, hi in zip(log_keeps, his)]
    survives = [_dot(jnp.concatenate([hi, lo], axis=1), suffix[...]) for hi, lo in zip(his, los)]
    partials = [lb + sv for lb, sv in zip(log_betas, survives)]
    totals = [jnp.sum(lk, axis=-1, keepdims=True) for lk in log_keeps]
    return partials, totals


def _sb_weights(partials, carries, mask):
    out = []
    for partial, carry in zip(partials, carries):
        a = jnp.exp(partial + carry)
        if mask is not None:
            a = jnp.where(mask, a, 0.0)
        out.append(a.astype(BF16))
    return out


def _attn_kernel(q_ref, k_ref, v_ref, km_ref, vm_ref, o_ref, acc_ref, suffix_ref, *, tq, hp,
                 n_meta, scale):
    qi = pl.program_id(2)
    suffix_ref[...] = _suffix_sum_matrix(tq)
    row = lax.broadcasted_iota(jnp.int32, (tq, tq), 0)
    col = lax.broadcasted_iota(jnp.int32, (tq, tq), 1)
    causal = col < row
    n_pad = km_ref.shape[0]
    meta_valid = lax.broadcasted_iota(jnp.int32, (tq, n_pad), 1) < n_meta
    lanes = [slice(g * HEAD_DIM, (g + 1) * HEAD_DIM) for g in range(hp)]

    def kv_blocks(ref, start):
        return [ref[pl.ds(start, tq), ln] for ln in lanes]

    start = pl.multiple_of(qi * tq, tq)
    partials, carries = _sb_partials(q_ref, lanes, kv_blocks(k_ref, start), suffix_ref, causal,
                                     scale)
    meta_partials, _ = _sb_partials(q_ref, lanes, [km_ref[:, ln] for ln in lanes],
                                    _suffix_sum_matrix(n_pad), meta_valid, scale)
    weights = _sb_weights(partials, [0.0] * hp, causal)
    for ln, a, v in zip(lanes, weights, kv_blocks(v_ref, start)):
        acc_ref[:, ln] = _dot(a, v)

    def body(i, carries):
        s = pl.multiple_of((qi - 1 - i) * tq, tq)
        partials, totals = _sb_partials(q_ref, lanes, kv_blocks(k_ref, s), suffix_ref, None, scale)
        weights = _sb_weights(partials, carries, None)
        for ln, a, v in zip(lanes, weights, kv_blocks(v_ref, s)):
            acc_ref[:, ln] += _dot(a, v)
        return tuple(c + t for c, t in zip(carries, totals))

    carries = lax.fori_loop(0, qi, body, tuple(carries))

    weights = _sb_weights(meta_partials, carries, meta_valid)
    for ln, a in zip(lanes, weights):
        o_ref[:, ln] = (acc_ref[:, ln] + _dot(a, vm_ref[:, ln])).astype(o_ref.dtype)


def _attention(q, kv, kv_meta, *, batch, seq, n_meta, tq, hp):
    m, d = q.shape
    width = hp * HEAD_DIM
    groups = d // width
    nq = seq // tq
    n_pad = kv_meta.shape[0]
    vmem = _vmem_budget(pipelined=2 * (tq + seq + n_pad) * width * 2,
                        resident=tq * width * 4 + 2 * tq * tq * 2, temps=hp * 16 * tq * tq * 4)
    return pl.pallas_call(
        functools.partial(_attn_kernel, tq=tq, hp=hp, n_meta=n_meta, scale=HEAD_DIM ** -0.5),
        grid=(batch, groups, nq),
        in_specs=[
            pl.BlockSpec((tq, width), lambda b, h, i: (b * nq + i, h)),
            pl.BlockSpec((seq, width), lambda b, h, i: (b, h)),
            pl.BlockSpec((seq, width), lambda b, h, i: (b, groups + h)),
            pl.BlockSpec((n_pad, width), lambda b, h, i: (0, h)),
            pl.BlockSpec((n_pad, width), lambda b, h, i: (0, groups + h)),
        ],
        out_specs=pl.BlockSpec((tq, width), lambda b, h, i: (b * nq + i, h)),
        out_shape=jax.ShapeDtypeStruct((m, d), BF16),
        scratch_shapes=[pltpu.VMEM((tq, width), F32), pltpu.VMEM((2 * tq, tq), BF16)],
        compiler_params=_compiler_params(("parallel", "parallel", "arbitrary"), vmem),
        name="sb_attention",
    )(q, kv, kv, kv_meta, kv_meta)


def _conv_halo(v_main, v_meta, *, batch, seq, tm):
    d = v_main.shape[1]
    nt = seq // tm
    tails = v_main.reshape(batch, nt, tm, d)[:, :nt - 1, tm - 2:, :]
    first = jnp.broadcast_to(v_meta[None, None, -2:, :], (batch, 1, 2, d))
    prev = jnp.concatenate([first, tails], axis=1).reshape(batch * nt, 2, d)
    return jnp.pad(prev, ((0, 0), (HALO_ROWS - 2, 0), (0, 0)))


def kernel(x, meta_tokens, ln_gain, ln_bias, ffn_w_in, ffn_w_out, conv_w_in, conv_w, conv_b,
           conv_w_out, sb_w_q, sb_w_kv, sb_w_o):
    batch, seq, d = x.shape
    n_meta = meta_tokens.shape[0]
    depth = ffn_w_in.shape[0]
    n_conv = conv_w_in.shape[0]
    alpha = float((2 * depth) ** 0.25)
    assert d % HEAD_DIM == 0 and n_meta % SUBLANES_BF16 == 0 and n_meta <= LANES

    tm = 512
    tf = 512
    tm_proj = 1024
    tn_proj = d
    tn_conv = 512
    tq = 256
    hp = 8

    w_in = ffn_w_in.astype(BF16)
    w_out = ffn_w_out.astype(BF16)
    cw_in = conv_w_in.astype(BF16)
    cw_out = conv_w_out.astype(BF16)
    w_q = sb_w_q.astype(BF16)
    w_kv = sb_w_kv.astype(BF16)[None]
    w_o = sb_w_o.astype(BF16)
    gain = ln_gain.reshape(depth, 3, 1, d)
    bias = ln_bias.reshape(depth, 3, 1, d)
    cb = conv_b.reshape(n_conv, 1, d)

    h = x.reshape(batch * seq, d)
    hb = h.astype(BF16)
    hm = meta_tokens.astype(x.dtype)
    hmb = hm.astype(BF16)
    kv = kv_meta = None
    for i in range(depth):
        meta_live = i < n_conv
        h, hb = _ffn_ln(h, hb, w_in, w_out, gain, bias, i, 0, 0, alpha=alpha, tm=tm, tf=tf)
        if meta_live:
            hm, hmb = _ffn_ln(hm, hmb, w_in, w_out, gain, bias, i, 0, 0, alpha=alpha, tm=n_meta,
                              tf=tf)
        if i < n_conv:
            bg, v = _conv_in(hb, cw_in, i, tm=tm_proj, tn=tn_conv)
            bgm, vm = _conv_in(hmb, cw_in, i, tm=n_meta, tn=tn_conv)
            halo = _conv_halo(v, vm, batch=batch, seq=seq, tm=tm)
            h, hb = _conv_out_ln(bg, v, halo, conv_w, cb, cw_out, h, gain, bias, i, i, 1,
                                 alpha=alpha, tm=tm)
            halo_m = jnp.zeros((1, HALO_ROWS, d), BF16)
            hm, hmb = _conv_out_ln(bgm, vm, halo_m, conv_w, cb, cw_out, hm, gain, bias, i, i, 1,
                                   alpha=alpha, tm=n_meta)
        else:
            j = i - n_conv
            q = _proj(hb, w_q, j, tm=tm_proj, tn=tn_proj)
            o = _attention(q, kv, kv_meta, batch=batch, seq=seq, n_meta=n_meta, tq=tq, hp=hp)
            h, hb = _mix_out_ln(o, w_o, h, gain, bias, j, i, 1, alpha=alpha, tm=tm)
        h, hb = _ffn_ln(h, hb, w_in, w_out, gain, bias, i, 1, 2, alpha=alpha, tm=tm, tf=tf)
        if meta_live:
            hm, hmb = _ffn_ln(hm, hmb, w_in, w_out, gain, bias, i, 1, 2, alpha=alpha, tm=n_meta,
                              tf=tf)
        if i == n_conv - 1:
            kv = _proj(hb, w_kv, 0, tm=tm_proj, tn=tn_proj)
            kvm = _proj(hmb, w_kv, 0, tm=n_meta, tn=tn_proj)
            kv_meta = jnp.pad(kvm, ((0, LANES - n_meta), (0, 0)))
    return h.reshape(batch, seq, d)
```

```python
import functools

import jax
import jax.numpy as jnp
from jax import lax
from jax.experimental import pallas as pl
from jax.experimental.pallas import tpu as pltpu

LN_EPS = 1e-5
HEAD_DIM = 128
LANES = 128
SUBLANES_BF16 = 16
HALO_ROWS = 8
MXU_ROWS_PER_WEIGHT_TILE = 256
VMEM_LIMIT_CAP = 60000 * 1024
COMPILER_SCRATCH_BYTES = 4 << 20

LOG2E = 1.4426950408889634

BF16 = jnp.bfloat16
F32 = jnp.float32


def _vmem_budget(*, pipelined, resident, temps):
    return 2 * pipelined + resident + temps + COMPILER_SCRATCH_BYTES


def _compiler_params(semantics, vmem_bytes):
    return pltpu.CompilerParams(
        dimension_semantics=semantics,
        vmem_limit_bytes=int(min(VMEM_LIMIT_CAP, vmem_bytes)),
    )


def _layer_norm(y, gain, bias):
    mu = jnp.mean(y, axis=-1, keepdims=True)
    yc = y - mu
    var = jnp.mean(yc * yc, axis=-1, keepdims=True)
    return yc * lax.rsqrt(var + LN_EPS) * gain + bias


def _dot(a, b):
    return jnp.dot(a, b, preferred_element_type=F32)


def _store_ln(y, g_ref, b_ref, o_ref, ob_ref):
    out = _layer_norm(y, g_ref[...], b_ref[...])
    o_ref[...] = out
    ob_ref[...] = out.astype(BF16)


def _ffn_kernel(x_ref, xb_ref, wg_ref, wu_ref, wo_ref, g_ref, b_ref, o_ref, ob_ref, *, alpha, nf):
    f = pl.program_id(1)

    @pl.when(f == 0)
    def _():
        o_ref[...] = (2.0 * alpha) * x_ref[...]

    xb = xb_ref[...]
    gate = _dot(xb, wg_ref[...])
    up = _dot(xb, wu_ref[...])
    act = (gate * jax.nn.sigmoid(gate) * up).astype(BF16)
    o_ref[...] += _dot(act, wo_ref[...])

    @pl.when(f == nf - 1)
    def _():
        _store_ln(0.5 * o_ref[...], g_ref, b_ref, o_ref, ob_ref)


def _ffn_ln(h, hb, w_in, w_out, gain, bias, li, lj, lk, *, alpha, tm, tf):
    m, d = h.shape
    ff = w_out.shape[2]
    nf = ff // tf
    vmem = _vmem_budget(pipelined=2 * tm * d * 4 + 2 * tm * d * 2 + 3 * d * tf * 2 + 2 * d * 4,
                        resident=0, temps=4 * tm * tf * 4 + 3 * tm * d * 4)
    return pl.pallas_call(
        functools.partial(_ffn_kernel, alpha=alpha, nf=nf),
        grid=(m // tm, nf),
        in_specs=[
            pl.BlockSpec((tm, d), lambda i, f: (i, 0)),
            pl.BlockSpec((tm, d), lambda i, f: (i, 0)),
            pl.BlockSpec((None, None, d, tf), lambda i, f: (li, lj, 0, f)),
            pl.BlockSpec((None, None, d, tf), lambda i, f: (li, lj, 0, nf + f)),
            pl.BlockSpec((None, None, tf, d), lambda i, f: (li, lj, f, 0)),
            pl.BlockSpec((None, None, 1, d), lambda i, f: (li, lk, 0, 0)),
            pl.BlockSpec((None, None, 1, d), lambda i, f: (li, lk, 0, 0)),
        ],
        out_specs=[pl.BlockSpec((tm, d), lambda i, f: (i, 0))] * 2,
        out_shape=[jax.ShapeDtypeStruct((m, d), F32), jax.ShapeDtypeStruct((m, d), BF16)],
        compiler_params=_compiler_params(("parallel", "arbitrary"), vmem),
        name="ffn_ln",
    )(h, hb, w_in, w_in, w_out, gain, bias)


def _proj_kernel(xb_ref, w_ref, o_ref):
    o_ref[...] = _dot(xb_ref[...], w_ref[...]).astype(o_ref.dtype)


def _proj(hb, w, li, *, tm, tn):
    m, d = hb.shape
    n = w.shape[2]
    vmem = _vmem_budget(pipelined=tm * d * 2 + d * tn * 2 + tm * tn * 2,
                        resident=0, temps=2 * tm * tn * 4)
    return pl.pallas_call(
        _proj_kernel,
        grid=(m // tm, n // tn),
        in_specs=[
            pl.BlockSpec((tm, d), lambda i, j: (i, 0)),
            pl.BlockSpec((None, d, tn), lambda i, j: (li, 0, j)),
        ],
        out_specs=pl.BlockSpec((tm, tn), lambda i, j: (i, j)),
        out_shape=jax.ShapeDtypeStruct((m, n), BF16),
        compiler_params=_compiler_params(("parallel", "arbitrary"), vmem),
        name="proj",
    )(hb, w)


def _conv_in_kernel(xb_ref, wb_ref, wc_ref, wu_ref, b_ref, v_ref):
    xb = xb_ref[...]
    b_ref[...] = _dot(xb, wb_ref[...]).astype(BF16)
    v_ref[...] = (_dot(xb, wc_ref[...]) * _dot(xb, wu_ref[...])).astype(BF16)


def _conv_in(hb, w_in, li, *, tm, tn):
    m, d = hb.shape
    nd = d // tn
    vmem = _vmem_budget(pipelined=tm * d * 2 + 3 * d * tn * 2 + 2 * tm * tn * 2,
                        resident=0, temps=4 * tm * tn * 4)
    return pl.pallas_call(
        _conv_in_kernel,
        grid=(m // tm, nd),
        in_specs=[
            pl.BlockSpec((tm, d), lambda i, j: (i, 0)),
            pl.BlockSpec((None, d, tn), lambda i, j: (li, 0, j)),
            pl.BlockSpec((None, d, tn), lambda i, j: (li, 0, nd + j)),
            pl.BlockSpec((None, d, tn), lambda i, j: (li, 0, 2 * nd + j)),
        ],
        out_specs=[
            pl.BlockSpec((tm, tn), lambda i, j: (i, j)),
            pl.BlockSpec((tm, tn), lambda i, j: (i, j)),
        ],
        out_shape=[jax.ShapeDtypeStruct((m, d), BF16), jax.ShapeDtypeStruct((m, d), BF16)],
        compiler_params=_compiler_params(("parallel", "arbitrary"), vmem),
        name="conv_in",
    )(hb, w_in, w_in, w_in)


def _row_chunks(tm):
    rc = min(tm, MXU_ROWS_PER_WEIGHT_TILE)
    return [slice(r, r + rc) for r in range(0, tm, rc)]


def _residual_ln_chunks(chunks, products, x_ref, g_ref, b_ref, o_ref, ob_ref, alpha):
    for rows, prod in zip(chunks, products):
        out = _layer_norm(alpha * x_ref[rows, :] + prod, g_ref[...], b_ref[...])
        o_ref[rows, :] = out
        ob_ref[rows, :] = out.astype(BF16)


def _mix_out_kernel(lhs_ref, w_ref, x_ref, g_ref, b_ref, o_ref, ob_ref, *, alpha):
    chunks = _row_chunks(x_ref.shape[0])
    products = [_dot(lhs_ref[rows, :], w_ref[...]) for rows in chunks]
    _residual_ln_chunks(chunks, products, x_ref, g_ref, b_ref, o_ref, ob_ref, alpha)


def _conv_out_kernel(bg_ref, v_ref, halo_ref, cw_ref, cb_ref, w_ref, x_ref, g_ref, b_ref,
                     o_ref, ob_ref, *, alpha):
    chunks = _row_chunks(x_ref.shape[0])
    cw = cw_ref[...]
    products = []
    for rows in chunks:
        v0 = v_ref[rows, :].astype(F32)
        if rows.start == 0:
            before = halo_ref[...].astype(F32)
        else:
            before = v_ref[rows.start - SUBLANES_BF16:rows.start, :].astype(F32)
        prev1 = before[-1:]
        prev2 = before[-2:-1]
        row = lax.broadcasted_iota(jnp.int32, (v0.shape[0], 1), 0)
        v1 = jnp.where(row == 0, prev1, pltpu.roll(v0, 1, axis=0))
        v2 = jnp.where(row == 0, prev2, jnp.where(row == 1, prev1, pltpu.roll(v0, 2, axis=0)))
        conv = v2 * cw[0:1] + v1 * cw[1:2] + v0 * cw[2:3] + cb_ref[...]
        lhs = (bg_ref[rows, :].astype(F32) * conv).astype(BF16)
        products.append(_dot(lhs, w_ref[...]))
    _residual_ln_chunks(chunks, products, x_ref, g_ref, b_ref, o_ref, ob_ref, alpha)


def _ln_out(m, d, tm):
    return dict(
        out_specs=[pl.BlockSpec((tm, d), lambda i: (i, 0))] * 2,
        out_shape=[jax.ShapeDtypeStruct((m, d), F32), jax.ShapeDtypeStruct((m, d), BF16)],
    )


def _mix_out_specs(tm, d, wi, li, lk):
    return [
        pl.BlockSpec((None, d, d), lambda i: (wi, 0, 0)),
        pl.BlockSpec((tm, d), lambda i: (i, 0)),
        pl.BlockSpec((None, None, 1, d), lambda i: (li, lk, 0, 0)),
        pl.BlockSpec((None, None, 1, d), lambda i: (li, lk, 0, 0)),
    ]


def _mix_out_ln(lhs, w, h, gain, bias, wi, li, lk, *, alpha, tm):
    m, d = h.shape
    vmem = _vmem_budget(pipelined=2 * tm * d * 2 + d * d * 2 + 2 * tm * d * 4 + 2 * d * 4,
                        resident=0, temps=4 * tm * d * 4)
    return pl.pallas_call(
        functools.partial(_mix_out_kernel, alpha=alpha),
        grid=(m // tm,),
        in_specs=[pl.BlockSpec((tm, d), lambda i: (i, 0))] + _mix_out_specs(tm, d, wi, li, lk),
        **_ln_out(m, d, tm),
        compiler_params=_compiler_params(("parallel",), vmem),
        name="mix_out_ln",
    )(lhs, w, h, gain, bias)


def _conv_out_ln(bg, v, halo, cw, cb, w, h, gain, bias, ci, li, lk, *, alpha, tm):
    m, d = h.shape
    taps = cw.shape[1]
    vmem = _vmem_budget(pipelined=3 * tm * d * 2 + d * d * 2 + 2 * tm * d * 4
                        + (HALO_ROWS + taps + 3) * d * 4, resident=0, temps=8 * tm * d * 4)
    return pl.pallas_call(
        functools.partial(_conv_out_kernel, alpha=alpha),
        grid=(m // tm,),
        in_specs=[
            pl.BlockSpec((tm, d), lambda i: (i, 0)),
            pl.BlockSpec((tm, d), lambda i: (i, 0)),
            pl.BlockSpec((None, HALO_ROWS, d), lambda i: (i, 0, 0)),
            pl.BlockSpec((None, taps, d), lambda i: (ci, 0, 0)),
            pl.BlockSpec((None, 1, d), lambda i: (ci, 0, 0)),
        ] + _mix_out_specs(tm, d, ci, li, lk),
        **_ln_out(m, d, tm),
        compiler_params=_compiler_params(("parallel",), vmem),
        name="conv_out_ln",
    )(bg, v, halo, cw, cb, w, h, gain, bias)


def _suffix_sum_matrix(n):
    r = lax.broadcasted_iota(jnp.int32, (2 * n, n), 0)
    c = lax.broadcasted_iota(jnp.int32, (2 * n, n), 1)
    return (jnp.where(r >= n, r - n, r) > c).astype(BF16)


def _sb_raw(q_ref, lanes, k_blks):
    return [lax.dot_general(q_ref[:, ln], k, (((1,), (1,)), ((), ())), preferred_element_type=F32)
            for ln, k in zip(lanes, k_blks)]


def _sb_logs(raws, mask, scale):
    log_betas, splits, totals = [], [], []
    for raw in raws:
        z = raw * scale
        soft = jnp.log(1.0 + jnp.exp2(jnp.abs(raw) * (-scale * LOG2E)))
        log_beta = jnp.minimum(z, 0.0) - soft
        log_keep = log_beta - z
        if mask is not None:
            log_keep = jnp.where(mask, log_keep, 0.0)
        hi = log_keep.astype(BF16)
        lo = (log_keep - hi.astype(F32)).astype(BF16)
        log_betas.append(log_beta)
        splits.append(jnp.concatenate([hi, lo], axis=1))
        totals.append(jnp.sum(log_keep, axis=-1, keepdims=True))
    return log_betas, splits, totals


def _sb_exp(log_beta, survive, carry, mask):
    a = jnp.exp(log_beta + survive + carry)
    if mask is not None:
        a = jnp.where(mask, a, 0.0)
    return a.astype(BF16)


def _sb_weights(log_betas, splits, suffix, carries, mask):
    survives = [_dot(split, suffix[...]) for split in splits]
    return [_sb_exp(lb, sv, c, mask) for lb, sv, c in zip(log_betas, survives, carries)]


def _attn_kernel(q_ref, k_ref, v_ref, km_ref, vm_ref, o_ref, acc_ref, suffix_ref, raw_ref, w_ref,
                 *, tq, hp, n_meta, scale):
    qi = pl.program_id(2)
    suffix_ref[...] = _suffix_sum_matrix(tq)
    row = lax.broadcasted_iota(jnp.int32, (tq, tq), 0)
    col = lax.broadcasted_iota(jnp.int32, (tq, tq), 1)
    causal = col < row
    n_pad = km_ref.shape[0]
    meta_valid = lax.broadcasted_iota(jnp.int32, (tq, n_pad), 1) < n_meta
    lanes = [slice(g * HEAD_DIM, (g + 1) * HEAD_DIM) for g in range(hp)]

    def kv_blocks(ref, block):
        start = pl.multiple_of(block * tq, tq)
        return [ref[pl.ds(start, tq), ln] for ln in lanes]

    def accumulate_pv(block):
        for g, (ln, v) in enumerate(zip(lanes, kv_blocks(v_ref, block))):
            acc_ref[:, ln] += _dot(w_ref[g], v)

    def visit(mask, pv_block, next_block, carries):
        next_k = kv_blocks(k_ref, next_block)
        prev_v = None if pv_block is None else kv_blocks(v_ref, pv_block)
        totals, pending = [], None
        for g, ln in enumerate(lanes):
            (log_beta,), (split,), (total,) = _sb_logs([raw_ref[g]], mask, scale)
            survive = _dot(split, suffix_ref[...])
            if prev_v is not None:
                acc_ref[:, ln] += _dot(w_ref[g], prev_v[g])
            raw_ref[g] = _sb_raw(q_ref, [ln], [next_k[g]])[0]
            if pending is not None:
                w_ref[g - 1] = _sb_exp(*pending, mask)
            pending = (log_beta, survive, carries[g])
            totals.append(total)
        w_ref[hp - 1] = _sb_exp(*pending, mask)
        return totals

    acc_ref[...] = jnp.zeros_like(acc_ref)
    for g, raw in enumerate(_sb_raw(q_ref, lanes, kv_blocks(k_ref, qi))):
        raw_ref[g] = raw
    carries = visit(causal, None, jnp.maximum(qi - 1, 0), [0.0] * hp)
    meta_logs = _sb_logs(_sb_raw(q_ref, lanes, [km_ref[:, ln] for ln in lanes]), meta_valid, scale)

    def body(t, carries):
        block = qi - t
        totals = visit(None, block + 1, jnp.maximum(block - 1, 0), carries)
        return tuple(c + tot for c, tot in zip(carries, totals))

    carries = lax.fori_loop(1, qi + 1, body, tuple(carries))

    accumulate_pv(0)
    log_betas, splits, _ = meta_logs
    weights = _sb_weights(log_betas, splits, _suffix_sum_matrix(n_pad), carries, meta_valid)
    for ln, a in zip(lanes, weights):
        o_ref[:, ln] = (acc_ref[:, ln] + _dot(a, vm_ref[:, ln])).astype(o_ref.dtype)


def _attention(q, kv, kv_meta, *, batch, seq, n_meta, tq, hp):
    m, d = q.shape
    width = hp * HEAD_DIM
    groups = d // width
    nq = seq // tq
    n_pad = kv_meta.shape[0]
    vmem = _vmem_budget(pipelined=2 * (tq + seq + n_pad) * width * 2,
                        resident=tq * width * 4 + 2 * tq * tq * 2 + hp * tq * tq * (4 + 2),
                        temps=hp * 16 * tq * tq * 4)
    return pl.pallas_call(
        functools.partial(_attn_kernel, tq=tq, hp=hp, n_meta=n_meta, scale=HEAD_DIM ** -0.5),
        grid=(batch, groups, nq),
        in_specs=[
            pl.BlockSpec((tq, width), lambda b, h, i: (b * nq + i, h)),
            pl.BlockSpec((seq, width), lambda b, h, i: (b, h)),
            pl.BlockSpec((seq, width), lambda b, h, i: (b, groups + h)),
            pl.BlockSpec((n_pad, width), lambda b, h, i: (0, h)),
            pl.BlockSpec((n_pad, width), lambda b, h, i: (0, groups + h)),
        ],
        out_specs=pl.BlockSpec((tq, width), lambda b, h, i: (b * nq + i, h)),
        out_shape=jax.ShapeDtypeStruct((m, d), BF16),
        scratch_shapes=[pltpu.VMEM((tq, width), F32), pltpu.VMEM((2 * tq, tq), BF16),
                        pltpu.VMEM((hp, tq, tq), F32), pltpu.VMEM((hp, tq, tq), BF16)],
        compiler_params=_compiler_params(("parallel", "parallel", "arbitrary"), vmem),
        name="sb_attention",
    )(q, kv, kv, kv_meta, kv_meta)


def _conv_halo(v_main, v_meta, *, batch, seq, tm):
    d = v_main.shape[1]
    nt = seq // tm
    tails = v_main.reshape(batch, nt, tm, d)[:, :nt - 1, tm - 2:, :]
    first = jnp.broadcast_to(v_meta[None, None, -2:, :], (batch, 1, 2, d))
    prev = jnp.concatenate([first, tails], axis=1).reshape(batch * nt, 2, d)
    return jnp.pad(prev, ((0, 0), (HALO_ROWS - 2, 0), (0, 0)))


def kernel(x, meta_tokens, ln_gain, ln_bias, ffn_w_in, ffn_w_out, conv_w_in, conv_w, conv_b,
           conv_w_out, sb_w_q, sb_w_kv, sb_w_o):
    batch, seq, d = x.shape
    n_meta = meta_tokens.shape[0]
    depth = ffn_w_in.shape[0]
    n_conv = conv_w_in.shape[0]
    alpha = float((2 * depth) ** 0.25)
    assert d % HEAD_DIM == 0 and n_meta % SUBLANES_BF16 == 0 and n_meta <= LANES

    tm = 512
    tf = 512
    tm_proj = 1024
    tn_proj = d
    tn_conv = 512
    tq = 256
    hp = 8

    w_in = ffn_w_in.astype(BF16)
    w_out = ffn_w_out.astype(BF16)
    cw_in = conv_w_in.astype(BF16)
    cw_out = conv_w_out.astype(BF16)
    w_q = sb_w_q.astype(BF16)
    w_kv = sb_w_kv.astype(BF16)[None]
    w_o = sb_w_o.astype(BF16)
    gain = ln_gain.reshape(depth, 3, 1, d)
    bias = ln_bias.reshape(depth, 3, 1, d)
    cb = conv_b.reshape(n_conv, 1, d)

    h = x.reshape(batch * seq, d)
    hb = h.astype(BF16)
    hm = meta_tokens.astype(x.dtype)
    hmb = hm.astype(BF16)
    kv = kv_meta = None
    for i in range(depth):
        meta_live = i < n_conv
        h, hb = _ffn_ln(h, hb, w_in, w_out, gain, bias, i, 0, 0, alpha=alpha, tm=tm, tf=tf)
        if meta_live:
            hm, hmb = _ffn_ln(hm, hmb, w_in, w_out, gain, bias, i, 0, 0, alpha=alpha, tm=n_meta,
                              tf=tf)
        if i < n_conv:
            bg, v = _conv_in(hb, cw_in, i, tm=tm_proj, tn=tn_conv)
            bgm, vm = _conv_in(hmb, cw_in, i, tm=n_meta, tn=tn_conv)
            halo = _conv_halo(v, vm, batch=batch, seq=seq, tm=tm)
            h, hb = _conv_out_ln(bg, v, halo, conv_w, cb, cw_out, h, gain, bias, i, i, 1,
                                 alpha=alpha, tm=tm)
            halo_m = jnp.zeros((1, HALO_ROWS, d), BF16)
            hm, hmb = _conv_out_ln(bgm, vm, halo_m, conv_w, cb, cw_out, hm, gain, bias, i, i, 1,
                                   alpha=alpha, tm=n_meta)
        else:
            j = i - n_conv
            q = _proj(hb, w_q, j, tm=tm_proj, tn=tn_proj)
            o = _attention(q, kv, kv_meta, batch=batch, seq=seq, n_meta=n_meta, tq=tq, hp=hp)
            h, hb = _mix_out_ln(o, w_o, h, gain, bias, j, i, 1, alpha=alpha, tm=tm)
        h, hb = _ffn_ln(h, hb, w_in, w_out, gain, bias, i, 1, 2, alpha=alpha, tm=tm, tf=tf)
        if meta_live:
            hm, hmb = _ffn_ln(hm, hmb, w_in, w_out, gain, bias, i, 1, 2, alpha=alpha, tm=n_meta,
                              tf=tf)
        if i == n_conv - 1:
            kv = _proj(hb, w_kv, 0, tm=tm_proj, tn=tn_proj)
            kvm = _proj(hmb, w_kv, 0, tm=n_meta, tn=tn_proj)
            kv_meta = jnp.pad(kvm, ((0, LANES - n_meta), (0, 0)))
    return h.reshape(batch, seq, d)
```

```python
import functools

import jax
import jax.numpy as jnp
from jax import lax
from jax.experimental import pallas as pl
from jax.experimental.pallas import tpu as pltpu

LN_EPS = 1e-5
HEAD_DIM = 128
LANES = 128
SUBLANES_BF16 = 16
HALO_ROWS = 8
MXU_ROWS_PER_WEIGHT_TILE = 256
VMEM_LIMIT_CAP = 60000 * 1024
COMPILER_SCRATCH_BYTES = 4 << 20

LOG2E = 1.4426950408889634

BF16 = jnp.bfloat16
F32 = jnp.float32


def _vmem_budget(*, pipelined, resident, temps):
    return 2 * pipelined + resident + temps + COMPILER_SCRATCH_BYTES


def _compiler_params(semantics, vmem_bytes):
    return pltpu.CompilerParams(
        dimension_semantics=semantics,
        vmem_limit_bytes=int(min(VMEM_LIMIT_CAP, vmem_bytes)),
    )


def _layer_norm(y, gain, bias):
    mu = jnp.mean(y, axis=-1, keepdims=True)
    yc = y - mu
    var = jnp.mean(yc * yc, axis=-1, keepdims=True)
    return yc * lax.rsqrt(var + LN_EPS) * gain + bias


def _dot(a, b):
    return jnp.dot(a, b, preferred_element_type=F32)


def _store_ln(y, g_ref, b_ref, o_ref, ob_ref):
    out = _layer_norm(y, g_ref[...], b_ref[...])
    o_ref[...] = out
    ob_ref[...] = out.astype(BF16)


def _ffn_kernel(x_ref, xb_ref, wgu_ref, wo_ref, g_ref, b_ref, o_ref, ob_ref, *, alpha, nf, tf):
    f = pl.program_id(1)
    tm = x_ref.shape[0]

    def branch(chunks):
        gus = [_dot(xb_ref[rows, :], wgu_ref[...]) for rows in chunks]
        acts = [(gu[:, :tf] * jax.nn.sigmoid(gu[:, :tf]) * gu[:, tf:]).astype(BF16) for gu in gus]
        return [_dot(act, wo_ref[...]) for act in acts]

    @pl.when(f == 0)
    def _():
        o_ref[...] = (2.0 * alpha) * x_ref[...] + branch([slice(0, tm)])[0]

    @pl.when(jnp.logical_and(f > 0, f < nf - 1))
    def _():
        o_ref[...] += branch([slice(0, tm)])[0]

    @pl.when(f == nf - 1)
    def _():
        chunks = _row_chunks(tm)
        for rows, prod in zip(chunks, branch(chunks)):
            out = _layer_norm(0.5 * (o_ref[rows, :] + prod), g_ref[...], b_ref[...])
            o_ref[rows, :] = out
            ob_ref[rows, :] = out.astype(BF16)


def _interleave_gate_up(w_in, tf):
    *lead, d, two_ff = w_in.shape
    w = w_in.reshape(*lead, d, 2, two_ff // (2 * tf), tf)
    return jnp.swapaxes(w, -3, -2).reshape(*lead, d, two_ff)


def _ffn_ln(h, hb, w_gu, w_out, gain, bias, li, lj, lk, *, alpha, tm, tf):
    m, d = h.shape
    ff = w_out.shape[2]
    nf = ff // tf
    assert nf >= 2
    vmem = _vmem_budget(pipelined=2 * tm * d * 4 + 2 * tm * d * 2 + 3 * d * tf * 2 + 2 * d * 4,
                        resident=0, temps=5 * tm * tf * 4 + 3 * tm * d * 4)
    return pl.pallas_call(
        functools.partial(_ffn_kernel, alpha=alpha, nf=nf, tf=tf),
        grid=(m // tm, nf),
        in_specs=[
            pl.BlockSpec((tm, d), lambda i, f: (i, 0)),
            pl.BlockSpec((tm, d), lambda i, f: (i, 0)),
            pl.BlockSpec((None, None, d, 2 * tf), lambda i, f: (li, lj, 0, f)),
            pl.BlockSpec((None, None, tf, d), lambda i, f: (li, lj, f, 0)),
            pl.BlockSpec((None, None, 1, d), lambda i, f: (li, lk, 0, 0)),
            pl.BlockSpec((None, None, 1, d), lambda i, f: (li, lk, 0, 0)),
        ],
        out_specs=[pl.BlockSpec((tm, d), lambda i, f: (i, 0))] * 2,
        out_shape=[jax.ShapeDtypeStruct((m, d), F32), jax.ShapeDtypeStruct((m, d), BF16)],
        compiler_params=_compiler_params(("parallel", "arbitrary"), vmem),
        name="ffn_ln",
    )(h, hb, w_gu, w_out, gain, bias)


def _proj_kernel(xb_ref, w_ref, o_ref):
    o_ref[...] = _dot(xb_ref[...], w_ref[...]).astype(o_ref.dtype)


def _proj(hb, w, li, *, tm, tn):
    m, d = hb.shape
    n = w.shape[2]
    vmem = _vmem_budget(pipelined=tm * d * 2 + d * tn * 2 + tm * tn * 2,
                        resident=0, temps=2 * tm * tn * 4)
    return pl.pallas_call(
        _proj_kernel,
        grid=(m // tm, n // tn),
        in_specs=[
            pl.BlockSpec((tm, d), lambda i, j: (i, 0)),
            pl.BlockSpec((None, d, tn), lambda i, j: (li, 0, j)),
        ],
        out_specs=pl.BlockSpec((tm, tn), lambda i, j: (i, j)),
        out_shape=jax.ShapeDtypeStruct((m, n), BF16),
        compiler_params=_compiler_params(("parallel", "arbitrary"), vmem),
        name="proj",
    )(hb, w)


def _conv_in_kernel(xb_ref, wb_ref, wc_ref, wu_ref, b_ref, v_ref):
    xb = xb_ref[...]
    b_ref[...] = _dot(xb, wb_ref[...]).astype(BF16)
    v_ref[...] = (_dot(xb, wc_ref[...]) * _dot(xb, wu_ref[...])).astype(BF16)


def _conv_in(hb, w_in, li, *, tm, tn):
    m, d = hb.shape
    nd = d // tn
    vmem = _vmem_budget(pipelined=tm * d * 2 + 3 * d * tn * 2 + 2 * tm * tn * 2,
                        resident=0, temps=4 * tm * tn * 4)
    return pl.pallas_call(
        _conv_in_kernel,
        grid=(m // tm, nd),
        in_specs=[
            pl.BlockSpec((tm, d), lambda i, j: (i, 0)),
            pl.BlockSpec((None, d, tn), lambda i, j: (li, 0, j)),
            pl.BlockSpec((None, d, tn), lambda i, j: (li, 0, nd + j)),
            pl.BlockSpec((None, d, tn), lambda i, j: (li, 0, 2 * nd + j)),
        ],
        out_specs=[
            pl.BlockSpec((tm, tn), lambda i, j: (i, j)),
            pl.BlockSpec((tm, tn), lambda i, j: (i, j)),
        ],
        out_shape=[jax.ShapeDtypeStruct((m, d), BF16), jax.ShapeDtypeStruct((m, d), BF16)],
        compiler_params=_compiler_params(("parallel", "arbitrary"), vmem),
        name="conv_in",
    )(hb, w_in, w_in, w_in)


def _row_chunks(tm):
    rc = min(tm, MXU_ROWS_PER_WEIGHT_TILE)
    return [slice(r, r + rc) for r in range(0, tm, rc)]


def _residual_ln_chunks(chunks, products, x_ref, g_ref, b_ref, o_ref, ob_ref, alpha):
    for rows, prod in zip(chunks, products):
        out = _layer_norm(alpha * x_ref[rows, :] + prod, g_ref[...], b_ref[...])
        o_ref[rows, :] = out
        ob_ref[rows, :] = out.astype(BF16)


def _mix_out_kernel(lhs_ref, w_ref, x_ref, g_ref, b_ref, o_ref, ob_ref, *, alpha):
    chunks = _row_chunks(x_ref.shape[0])
    products = [_dot(lhs_ref[rows, :], w_ref[...]) for rows in chunks]
    _residual_ln_chunks(chunks, products, x_ref, g_ref, b_ref, o_ref, ob_ref, alpha)


def _conv_out_kernel(bg_ref, v_ref, halo_ref, cw_ref, cb_ref, w_ref, x_ref, g_ref, b_ref,
                     o_ref, ob_ref, *, alpha):
    chunks = _row_chunks(x_ref.shape[0])
    cw = cw_ref[...]
    products = []
    for rows in chunks:
        v0 = v_ref[rows, :].astype(F32)
        if rows.start == 0:
            before = halo_ref[...].astype(F32)
        else:
            before = v_ref[rows.start - SUBLANES_BF16:rows.start, :].astype(F32)
        prev1 = before[-1:]
        prev2 = before[-2:-1]
        row = lax.broadcasted_iota(jnp.int32, (v0.shape[0], 1), 0)
        v1 = jnp.where(row == 0, prev1, pltpu.roll(v0, 1, axis=0))
        v2 = jnp.where(row == 0, prev2, jnp.where(row == 1, prev1, pltpu.roll(v0, 2, axis=0)))
        conv = v2 * cw[0:1] + v1 * cw[1:2] + v0 * cw[2:3] + cb_ref[...]
        lhs = (bg_ref[rows, :].astype(F32) * conv).astype(BF16)
        products.append(_dot(lhs, w_ref[...]))
    _residual_ln_chunks(chunks, products, x_ref, g_ref, b_ref, o_ref, ob_ref, alpha)


def _ln_out(m, d, tm):
    return dict(
        out_specs=[pl.BlockSpec((tm, d), lambda i: (i, 0))] * 2,
        out_shape=[jax.ShapeDtypeStruct((m, d), F32), jax.ShapeDtypeStruct((m, d), BF16)],
    )


def _mix_out_specs(tm, d, wi, li, lk):
    return [
        pl.BlockSpec((None, d, d), lambda i: (wi, 0, 0)),
        pl.BlockSpec((tm, d), lambda i: (i, 0)),
        pl.BlockSpec((None, None, 1, d), lambda i: (li, lk, 0, 0)),
        pl.BlockSpec((None, None, 1, d), lambda i: (li, lk, 0, 0)),
    ]


def _mix_out_ln(lhs, w, h, gain, bias, wi, li, lk, *, alpha, tm):
    m, d = h.shape
    vmem = _vmem_budget(pipelined=2 * tm * d * 2 + d * d * 2 + 2 * tm * d * 4 + 2 * d * 4,
                        resident=0, temps=4 * tm * d * 4)
    return pl.pallas_call(
        functools.partial(_mix_out_kernel, alpha=alpha),
        grid=(m // tm,),
        in_specs=[pl.BlockSpec((tm, d), lambda i: (i, 0))] + _mix_out_specs(tm, d, wi, li, lk),
        **_ln_out(m, d, tm),
        compiler_params=_compiler_params(("parallel",), vmem),
        name="mix_out_ln",
    )(lhs, w, h, gain, bias)


def _conv_out_ln(bg, v, halo, cw, cb, w, h, gain, bias, ci, li, lk, *, alpha, tm):
    m, d = h.shape
    taps = cw.shape[1]
    vmem = _vmem_budget(pipelined=3 * tm * d * 2 + d * d * 2 + 2 * tm * d * 4
                        + (HALO_ROWS + taps + 3) * d * 4, resident=0, temps=8 * tm * d * 4)
    return pl.pallas_call(
        functools.partial(_conv_out_kernel, alpha=alpha),
        grid=(m // tm,),
        in_specs=[
            pl.BlockSpec((tm, d), lambda i: (i, 0)),
            pl.BlockSpec((tm, d), lambda i: (i, 0)),
            pl.BlockSpec((None, HALO_ROWS, d), lambda i: (i, 0, 0)),
            pl.BlockSpec((None, taps, d), lambda i: (ci, 0, 0)),
            pl.BlockSpec((None, 1, d), lambda i: (ci, 0, 0)),
        ] + _mix_out_specs(tm, d, ci, li, lk),
        **_ln_out(m, d, tm),
        compiler_params=_compiler_params(("parallel",), vmem),
        name="conv_out_ln",
    )(bg, v, halo, cw, cb, w, h, gain, bias)


def _suffix_sum_matrix(n):
    r = lax.broadcasted_iota(jnp.int32, (2 * n, n), 0)
    c = lax.broadcasted_iota(jnp.int32, (2 * n, n), 1)
    return (jnp.where(r >= n, r - n, r) > c).astype(BF16)


def _sb_raw(q_ref, lanes, k_blks):
    return [lax.dot_general(q_ref[:, ln], k, (((1,), (1,)), ((), ())), preferred_element_type=F32)
            for ln, k in zip(lanes, k_blks)]


def _sb_logs(raws, mask, scale):
    log_betas, splits, totals = [], [], []
    for raw in raws:
        z = raw * scale
        soft = jnp.log(1.0 + jnp.exp2(jnp.abs(raw) * (-scale * LOG2E)))
        log_beta = jnp.minimum(z, 0.0) - soft
        log_keep = log_beta - z
        if mask is not None:
            log_keep = jnp.where(mask, log_keep, 0.0)
        hi = log_keep.astype(BF16)
        lo = (log_keep - hi.astype(F32)).astype(BF16)
        log_betas.append(log_beta)
        splits.append(jnp.concatenate([hi, lo], axis=1))
        totals.append(jnp.sum(log_keep, axis=-1, keepdims=True))
    return log_betas, splits, totals


def _sb_exp(log_beta, survive, carry, mask):
    a = jnp.exp(log_beta + survive + carry)
    if mask is not None:
        a = jnp.where(mask, a, 0.0)
    return a.astype(BF16)


def _sb_weights(log_betas, splits, suffix, carries, mask):
    survives = [_dot(split, suffix[...]) for split in splits]
    return [_sb_exp(lb, sv, c, mask) for lb, sv, c in zip(log_betas, survives, carries)]


def _attn_kernel(q_ref, k_ref, v_ref, km_ref, vm_ref, o_ref, acc_ref, suffix_ref, raw_ref, w_ref,
                 *, tq, hp, n_meta, scale):
    qi = pl.program_id(2)
    suffix_ref[...] = _suffix_sum_matrix(tq)
    row = lax.broadcasted_iota(jnp.int32, (tq, tq), 0)
    col = lax.broadcasted_iota(jnp.int32, (tq, tq), 1)
    causal = col < row
    n_pad = km_ref.shape[0]
    meta_valid = lax.broadcasted_iota(jnp.int32, (tq, n_pad), 1) < n_meta
    lanes = [slice(g * HEAD_DIM, (g + 1) * HEAD_DIM) for g in range(hp)]

    def kv_blocks(ref, block):
        start = pl.multiple_of(block * tq, tq)
        return [ref[pl.ds(start, tq), ln] for ln in lanes]

    def accumulate_pv(block):
        for g, (ln, v) in enumerate(zip(lanes, kv_blocks(v_ref, block))):
            acc_ref[:, ln] += _dot(w_ref[g], v)

    def visit(mask, pv_block, next_block, carries):
        next_k = kv_blocks(k_ref, next_block)
        prev_v = None if pv_block is None else kv_blocks(v_ref, pv_block)
        totals, pending = [], None
        for g, ln in enumerate(lanes):
            (log_beta,), (split,), (total,) = _sb_logs([raw_ref[g]], mask, scale)
            survive = _dot(split, suffix_ref[...])
            if prev_v is not None:
                acc_ref[:, ln] += _dot(w_ref[g], prev_v[g])
            raw_ref[g] = _sb_raw(q_ref, [ln], [next_k[g]])[0]
            if pending is not None:
                w_ref[g - 1] = _sb_exp(*pending, mask)
            pending = (log_beta, survive, carries[g])
            totals.append(total)
        w_ref[hp - 1] = _sb_exp(*pending, mask)
        return totals

    acc_ref[...] = jnp.zeros_like(acc_ref)
    for g, raw in enumerate(_sb_raw(q_ref, lanes, kv_blocks(k_ref, qi))):
        raw_ref[g] = raw
    carries = visit(causal, None, jnp.maximum(qi - 1, 0), [0.0] * hp)
    meta_logs = _sb_logs(_sb_raw(q_ref, lanes, [km_ref[:, ln] for ln in lanes]), meta_valid, scale)

    def body(t, carries):
        block = qi - t
        totals = visit(None, block + 1, jnp.maximum(block - 1, 0), carries)
        return tuple(c + tot for c, tot in zip(carries, totals))

    carries = lax.fori_loop(1, qi + 1, body, tuple(carries))

    accumulate_pv(0)
    log_betas, splits, _ = meta_logs
    weights = _sb_weights(log_betas, splits, _suffix_sum_matrix(n_pad), carries, meta_valid)
    for ln, a in zip(lanes, weights):
        o_ref[:, ln] = (acc_ref[:, ln] + _dot(a, vm_ref[:, ln])).astype(o_ref.dtype)


def _attention(q, kv, kv_meta, *, batch, seq, n_meta, tq, hp):
    m, d = q.shape
    width = hp * HEAD_DIM
    groups = d // width
    nq = seq // tq
    n_pad = kv_meta.shape[0]
    vmem = _vmem_budget(pipelined=2 * (tq + seq + n_pad) * width * 2,
                        resident=tq * width * 4 + 2 * tq * tq * 2 + hp * tq * tq * (4 + 2),
                        temps=hp * 16 * tq * tq * 4)
    return pl.pallas_call(
        functools.partial(_attn_kernel, tq=tq, hp=hp, n_meta=n_meta, scale=HEAD_DIM ** -0.5),
        grid=(batch, groups, nq),
        in_specs=[
            pl.BlockSpec((tq, width), lambda b, h, i: (b * nq + i, h)),
            pl.BlockSpec((seq, width), lambda b, h, i: (b, h)),
            pl.BlockSpec((seq, width), lambda b, h, i: (b, groups + h)),
            pl.BlockSpec((n_pad, width), lambda b, h, i: (0, h)),
            pl.BlockSpec((n_pad, width), lambda b, h, i: (0, groups + h)),
        ],
        out_specs=pl.BlockSpec((tq, width), lambda b, h, i: (b * nq + i, h)),
        out_shape=jax.ShapeDtypeStruct((m, d), BF16),
        scratch_shapes=[pltpu.VMEM((tq, width), F32), pltpu.VMEM((2 * tq, tq), BF16),
                        pltpu.VMEM((hp, tq, tq), F32), pltpu.VMEM((hp, tq, tq), BF16)],
        compiler_params=_compiler_params(("parallel", "parallel", "arbitrary"), vmem),
        name="sb_attention",
    )(q, kv, kv, kv_meta, kv_meta)


def _conv_halo(v_main, v_meta, *, batch, seq, tm):
    d = v_main.shape[1]
    nt = seq // tm
    tails = v_main.reshape(batch, nt, tm, d)[:, :nt - 1, tm - 2:, :]
    first = jnp.broadcast_to(v_meta[None, None, -2:, :], (batch, 1, 2, d))
    prev = jnp.concatenate([first, tails], axis=1).reshape(batch * nt, 2, d)
    return jnp.pad(prev, ((0, 0), (HALO_ROWS - 2, 0), (0, 0)))


def kernel(x, meta_tokens, ln_gain, ln_bias, ffn_w_in, ffn_w_out, conv_w_in, conv_w, conv_b,
           conv_w_out, sb_w_q, sb_w_kv, sb_w_o):
    batch, seq, d = x.shape
    n_meta = meta_tokens.shape[0]
    depth = ffn_w_in.shape[0]
    n_conv = conv_w_in.shape[0]
    alpha = float((2 * depth) ** 0.25)
    assert d % HEAD_DIM == 0 and n_meta % SUBLANES_BF16 == 0 and n_meta <= LANES

    tm = 512
    tf = 512
    tm_proj = 1024
    tn_proj = d
    tn_conv = 512
    tq = 256
    hp = 8

    w_in = _interleave_gate_up(ffn_w_in.astype(BF16), tf)
    w_out = ffn_w_out.astype(BF16)
    cw_in = conv_w_in.astype(BF16)
    cw_out = conv_w_out.astype(BF16)
    w_q = sb_w_q.astype(BF16)
    w_kv = sb_w_kv.astype(BF16)[None]
    w_o = sb_w_o.astype(BF16)
    gain = ln_gain.reshape(depth, 3, 1, d)
    bias = ln_bias.reshape(depth, 3, 1, d)
    cb = conv_b.reshape(n_conv, 1, d)

    h = x.reshape(batch * seq, d)
    hb = h.astype(BF16)
    hm = meta_tokens.astype(x.dtype)
    hmb = hm.astype(BF16)
    kv = kv_meta = None
    for i in range(depth):
        meta_live = i < n_conv
        h, hb = _ffn_ln(h, hb, w_in, w_out, gain, bias, i, 0, 0, alpha=alpha, tm=tm, tf=tf)
        if meta_live:
            hm, hmb = _ffn_ln(hm, hmb, w_in, w_out, gain, bias, i, 0, 0, alpha=alpha, tm=n_meta,
                              tf=tf)
        if i < n_conv:
            bg, v = _conv_in(hb, cw_in, i, tm=tm_proj, tn=tn_conv)
            bgm, vm = _conv_in(hmb, cw_in, i, tm=n_meta, tn=tn_conv)
            halo = _conv_halo(v, vm, batch=batch, seq=seq, tm=tm)
            h, hb = _conv_out_ln(bg, v, halo, conv_w, cb, cw_out, h, gain, bias, i, i, 1,
                                 alpha=alpha, tm=tm)
            halo_m = jnp.zeros((1, HALO_ROWS, d), BF16)
            hm, hmb = _conv_out_ln(bgm, vm, halo_m, conv_w, cb, cw_out, hm, gain, bias, i, i, 1,
                                   alpha=alpha, tm=n_meta)
        else:
            j = i - n_conv
            q = _proj(hb, w_q, j, tm=tm_proj, tn=tn_proj)
            o = _attention(q, kv, kv_meta, batch=batch, seq=seq, n_meta=n_meta, tq=tq, hp=hp)
            h, hb = _mix_out_ln(o, w_o, h, gain, bias, j, i, 1, alpha=alpha, tm=tm)
        h, hb = _ffn_ln(h, hb, w_in, w_out, gain, bias, i, 1, 2, alpha=alpha, tm=tm, tf=tf)
        if meta_live:
            hm, hmb = _ffn_ln(hm, hmb, w_in, w_out, gain, bias, i, 1, 2, alpha=alpha, tm=n_meta,
                              tf=tf)
        if i == n_conv - 1:
            kv = _proj(hb, w_kv, 0, tm=tm_proj, tn=tn_proj)
            kvm = _proj(hmb, w_kv, 0, tm=n_meta, tn=tn_proj)
            kv_meta = jnp.pad(kvm, ((0, LANES - n_meta), (0, 0)))
    return h.reshape(batch, seq, d)
```

```python
import functools

import jax
import jax.numpy as jnp
from jax import lax
from jax.experimental import pallas as pl
from jax.experimental.pallas import tpu as pltpu

LN_EPS = 1e-5
HEAD_DIM = 128
LANES = 128
SUBLANES_BF16 = 16
HALO_ROWS = 8
MXU_ROWS_PER_WEIGHT_TILE = 256
VMEM_LIMIT_CAP = 60000 * 1024
COMPILER_SCRATCH_BYTES = 4 << 20

LOG2E = 1.4426950408889634

BF16 = jnp.bfloat16
F32 = jnp.float32


def _vmem_budget(*, pipelined, resident, temps):
    return 2 * pipelined + resident + temps + COMPILER_SCRATCH_BYTES


def _compiler_params(semantics, vmem_bytes):
    return pltpu.CompilerParams(
        dimension_semantics=semantics,
        vmem_limit_bytes=int(min(VMEM_LIMIT_CAP, vmem_bytes)),
    )


def _layer_norm(y, gain, bias):
    mu = jnp.mean(y, axis=-1, keepdims=True)
    yc = y - mu
    var = jnp.mean(yc * yc, axis=-1, keepdims=True)
    return yc * lax.rsqrt(var + LN_EPS) * gain + bias


def _dot(a, b):
    return jnp.dot(a, b, preferred_element_type=F32)


def _ffn_kernel(x_ref, xb_ref, wg_ref, wu_ref, wo_ref, g_ref, b_ref, o_ref, ob_ref, *, alpha, nf):
    f = pl.program_id(1)
    tm = x_ref.shape[0]

    def branch(chunks):
        gates = [_dot(xb_ref[rows, :], wg_ref[...]) for rows in chunks]
        ups = [_dot(xb_ref[rows, :], wu_ref[...]) for rows in chunks]
        acts = [(g * jax.nn.sigmoid(g) * u).astype(BF16) for g, u in zip(gates, ups)]
        return [_dot(act, wo_ref[...]) for act in acts]

    @pl.when(f == 0)
    def _():
        o_ref[...] = (2.0 * alpha) * x_ref[...] + branch([slice(0, tm)])[0]

    @pl.when(jnp.logical_and(f > 0, f < nf - 1))
    def _():
        o_ref[...] += branch([slice(0, tm)])[0]

    @pl.when(f == nf - 1)
    def _():
        chunks = _row_chunks(tm)
        for rows, prod in zip(chunks, branch(chunks)):
            out = _layer_norm(0.5 * (o_ref[rows, :] + prod), g_ref[...], b_ref[...])
            o_ref[rows, :] = out
            ob_ref[rows, :] = out.astype(BF16)


def _ffn_ln(h, hb, w_in, w_out, gain, bias, li, lj, lk, *, alpha, tm, tf):
    m, d = h.shape
    ff = w_out.shape[2]
    nf = ff // tf
    assert nf >= 2
    vmem = _vmem_budget(pipelined=2 * tm * d * 4 + 2 * tm * d * 2 + 3 * d * tf * 2 + 2 * d * 4,
                        resident=0, temps=5 * tm * tf * 4 + 3 * tm * d * 4)
    return pl.pallas_call(
        functools.partial(_ffn_kernel, alpha=alpha, nf=nf),
        grid=(m // tm, nf),
        in_specs=[
            pl.BlockSpec((tm, d), lambda i, f: (i, 0)),
            pl.BlockSpec((tm, d), lambda i, f: (i, 0)),
            pl.BlockSpec((None, None, d, tf), lambda i, f: (li, lj, 0, f)),
            pl.BlockSpec((None, None, d, tf), lambda i, f: (li, lj, 0, nf + f)),
            pl.BlockSpec((None, None, tf, d), lambda i, f: (li, lj, f, 0)),
            pl.BlockSpec((None, None, 1, d), lambda i, f: (li, lk, 0, 0)),
            pl.BlockSpec((None, None, 1, d), lambda i, f: (li, lk, 0, 0)),
        ],
        out_specs=[pl.BlockSpec((tm, d), lambda i, f: (i, 0))] * 2,
        out_shape=[jax.ShapeDtypeStruct((m, d), F32), jax.ShapeDtypeStruct((m, d), BF16)],
        compiler_params=_compiler_params(("parallel", "arbitrary"), vmem),
        name="ffn_ln",
    )(h, hb, w_in, w_in, w_out, gain, bias)


def _proj_kernel(xb_ref, w_ref, o_ref, *, out_scale):
    y = _dot(xb_ref[...], w_ref[...])
    if out_scale != 1.0:
        y = y * out_scale
    o_ref[...] = y.astype(o_ref.dtype)


def _proj(hb, w, li, *, tm, tn, out_scale=1.0):
    m, d = hb.shape
    n = w.shape[2]
    vmem = _vmem_budget(pipelined=tm * d * 2 + d * tn * 2 + tm * tn * 2,
                        resident=0, temps=2 * tm * tn * 4)
    return pl.pallas_call(
        functools.partial(_proj_kernel, out_scale=out_scale),
        grid=(m // tm, n // tn),
        in_specs=[
            pl.BlockSpec((tm, d), lambda i, j: (i, 0)),
            pl.BlockSpec((None, d, tn), lambda i, j: (li, 0, j)),
        ],
        out_specs=pl.BlockSpec((tm, tn), lambda i, j: (i, j)),
        out_shape=jax.ShapeDtypeStruct((m, n), BF16),
        compiler_params=_compiler_params(("parallel", "arbitrary"), vmem),
        name="proj",
    )(hb, w)


def _conv_in_kernel(xb_ref, wb_ref, wc_ref, wu_ref, b_ref, v_ref):
    xb = xb_ref[...]
    b_ref[...] = _dot(xb, wb_ref[...]).astype(BF16)
    v_ref[...] = (_dot(xb, wc_ref[...]) * _dot(xb, wu_ref[...])).astype(BF16)


def _conv_in(hb, w_in, li, *, tm, tn):
    m, d = hb.shape
    nd = d // tn
    vmem = _vmem_budget(pipelined=tm * d * 2 + 3 * d * tn * 2 + 2 * tm * tn * 2,
                        resident=0, temps=4 * tm * tn * 4)
    return pl.pallas_call(
        _conv_in_kernel,
        grid=(m // tm, nd),
        in_specs=[
            pl.BlockSpec((tm, d), lambda i, j: (i, 0)),
            pl.BlockSpec((None, d, tn), lambda i, j: (li, 0, j)),
            pl.BlockSpec((None, d, tn), lambda i, j: (li, 0, nd + j)),
            pl.BlockSpec((None, d, tn), lambda i, j: (li, 0, 2 * nd + j)),
        ],
        out_specs=[
            pl.BlockSpec((tm, tn), lambda i, j: (i, j)),
            pl.BlockSpec((tm, tn), lambda i, j: (i, j)),
        ],
        out_shape=[jax.ShapeDtypeStruct((m, d), BF16), jax.ShapeDtypeStruct((m, d), BF16)],
        compiler_params=_compiler_params(("parallel", "arbitrary"), vmem),
        name="conv_in",
    )(hb, w_in, w_in, w_in)


def _row_chunks(tm):
    rc = min(tm, MXU_ROWS_PER_WEIGHT_TILE)
    return [slice(r, r + rc) for r in range(0, tm, rc)]


def _residual_ln_chunks(chunks, products, x_ref, g_ref, b_ref, o_ref, ob_ref, alpha):
    for rows, prod in zip(chunks, products):
        out = _layer_norm(alpha * x_ref[rows, :] + prod, g_ref[...], b_ref[...])
        o_ref[rows, :] = out
        ob_ref[rows, :] = out.astype(BF16)


def _mix_out_kernel(lhs_ref, w_ref, x_ref, g_ref, b_ref, o_ref, ob_ref, *, alpha):
    chunks = _row_chunks(x_ref.shape[0])
    products = [_dot(lhs_ref[rows, :], w_ref[...]) for rows in chunks]
    _residual_ln_chunks(chunks, products, x_ref, g_ref, b_ref, o_ref, ob_ref, alpha)


def _conv_out_kernel(bg_ref, v_ref, halo_ref, cw_ref, cb_ref, w_ref, x_ref, g_ref, b_ref,
                     o_ref, ob_ref, *, alpha):
    chunks = _row_chunks(x_ref.shape[0])
    cw = cw_ref[...]
    products = []
    for rows in chunks:
        v0 = v_ref[rows, :].astype(F32)
        if rows.start == 0:
            before = halo_ref[...].astype(F32)
        else:
            before = v_ref[rows.start - SUBLANES_BF16:rows.start, :].astype(F32)
        prev1 = before[-1:]
        prev2 = before[-2:-1]
        row = lax.broadcasted_iota(jnp.int32, (v0.shape[0], 1), 0)
        v1 = jnp.where(row == 0, prev1, pltpu.roll(v0, 1, axis=0))
        v2 = jnp.where(row == 0, prev2, jnp.where(row == 1, prev1, pltpu.roll(v0, 2, axis=0)))
        conv = v2 * cw[0:1] + v1 * cw[1:2] + v0 * cw[2:3] + cb_ref[...]
        lhs = (bg_ref[rows, :].astype(F32) * conv).astype(BF16)
        products.append(_dot(lhs, w_ref[...]))
    _residual_ln_chunks(chunks, products, x_ref, g_ref, b_ref, o_ref, ob_ref, alpha)


def _ln_out(m, d, tm):
    return dict(
        out_specs=[pl.BlockSpec((tm, d), lambda i: (i, 0))] * 2,
        out_shape=[jax.ShapeDtypeStruct((m, d), F32), jax.ShapeDtypeStruct((m, d), BF16)],
    )


def _mix_out_specs(tm, d, wi, li, lk):
    return [
        pl.BlockSpec((None, d, d), lambda i: (wi, 0, 0)),
        pl.BlockSpec((tm, d), lambda i: (i, 0)),
        pl.BlockSpec((None, None, 1, d), lambda i: (li, lk, 0, 0)),
        pl.BlockSpec((None, None, 1, d), lambda i: (li, lk, 0, 0)),
    ]


def _mix_out_ln(lhs, w, h, gain, bias, wi, li, lk, *, alpha, tm):
    m, d = h.shape
    vmem = _vmem_budget(pipelined=2 * tm * d * 2 + d * d * 2 + 2 * tm * d * 4 + 2 * d * 4,
                        resident=0, temps=4 * tm * d * 4)
    return pl.pallas_call(
        functools.partial(_mix_out_kernel, alpha=alpha),
        grid=(m // tm,),
        in_specs=[pl.BlockSpec((tm, d), lambda i: (i, 0))] + _mix_out_specs(tm, d, wi, li, lk),
        **_ln_out(m, d, tm),
        compiler_params=_compiler_params(("parallel",), vmem),
        name="mix_out_ln",
    )(lhs, w, h, gain, bias)


def _conv_out_ln(bg, v, halo, cw, cb, w, h, gain, bias, ci, li, lk, *, alpha, tm):
    m, d = h.shape
    taps = cw.shape[1]
    vmem = _vmem_budget(pipelined=3 * tm * d * 2 + d * d * 2 + 2 * tm * d * 4
                        + (HALO_ROWS + taps + 3) * d * 4, resident=0, temps=8 * tm * d * 4)
    return pl.pallas_call(
        functools.partial(_conv_out_kernel, alpha=alpha),
        grid=(m // tm,),
        in_specs=[
            pl.BlockSpec((tm, d), lambda i: (i, 0)),
            pl.BlockSpec((tm, d), lambda i: (i, 0)),
            pl.BlockSpec((None, HALO_ROWS, d), lambda i: (i, 0, 0)),
            pl.BlockSpec((None, taps, d), lambda i: (ci, 0, 0)),
            pl.BlockSpec((None, 1, d), lambda i: (ci, 0, 0)),
        ] + _mix_out_specs(tm, d, ci, li, lk),
        **_ln_out(m, d, tm),
        compiler_params=_compiler_params(("parallel",), vmem),
        name="conv_out_ln",
    )(bg, v, halo, cw, cb, w, h, gain, bias)


def _suffix_sum_matrix(n):
    r = lax.broadcasted_iota(jnp.int32, (2 * n, n), 0)
    c = lax.broadcasted_iota(jnp.int32, (2 * n, n), 1)
    return (jnp.where(r >= n, r - n, r) > c).astype(BF16)


def _sb_raw(q_ref, lanes, k_blks):
    return [lax.dot_general(q_ref[:, ln], k, (((1,), (1,)), ((), ())), preferred_element_type=F32)
            for ln, k in zip(lanes, k_blks)]


def _sb_logs(zs, mask):
    log_betas, splits, totals = [], [], []
    for z in zs:
        soft = jnp.log(1.0 + jnp.exp2(jnp.abs(z) * -LOG2E))
        log_beta = jnp.minimum(z, 0.0) - soft
        log_keep = log_beta - z
        if mask is not None:
            log_keep = jnp.where(mask, log_keep, 0.0)
        hi = log_keep.astype(BF16)
        lo = (log_keep - hi.astype(F32)).astype(BF16)
        log_betas.append(log_beta)
        splits.append(jnp.concatenate([hi, lo], axis=1))
        totals.append(jnp.sum(log_keep, axis=-1, keepdims=True))
    return log_betas, splits, totals


def _sb_exp(log_beta, survive, carry, mask):
    a = jnp.exp(log_beta + survive + carry)
    if mask is not None:
        a = jnp.where(mask, a, 0.0)
    return a.astype(BF16)


def _sb_weights(log_betas, splits, suffix, carries, mask):
    survives = [_dot(split, suffix[...]) for split in splits]
    return [_sb_exp(lb, sv, c, mask) for lb, sv, c in zip(log_betas, survives, carries)]


def _attn_kernel(q_ref, k_ref, v_ref, km_ref, vm_ref, o_ref, acc_ref, suffix_ref, raw_ref, w_ref,
                 *, tq, hp, n_meta):
    qi = pl.program_id(2)
    suffix_ref[...] = _suffix_sum_matrix(tq)
    row = lax.broadcasted_iota(jnp.int32, (tq, tq), 0)
    col = lax.broadcasted_iota(jnp.int32, (tq, tq), 1)
    causal = col < row
    n_pad = km_ref.shape[0]
    meta_valid = lax.broadcasted_iota(jnp.int32, (tq, n_pad), 1) < n_meta
    lanes = [slice(g * HEAD_DIM, (g + 1) * HEAD_DIM) for g in range(hp)]

    def kv_blocks(ref, block):
        start = pl.multiple_of(block * tq, tq)
        return [ref[pl.ds(start, tq), ln] for ln in lanes]

    def accumulate_pv(block):
        for g, (ln, v) in enumerate(zip(lanes, kv_blocks(v_ref, block))):
            acc_ref[:, ln] += _dot(w_ref[g], v)

    def visit(mask, pv_block, next_block, carries):
        next_k = kv_blocks(k_ref, next_block)
        prev_v = None if pv_block is None else kv_blocks(v_ref, pv_block)
        totals, pending = [], None
        for g, ln in enumerate(lanes):
            (log_beta,), (split,), (total,) = _sb_logs([raw_ref[g]], mask)
            survive = _dot(split, suffix_ref[...])
            if prev_v is not None:
                acc_ref[:, ln] += _dot(w_ref[g], prev_v[g])
            raw_ref[g] = _sb_raw(q_ref, [ln], [next_k[g]])[0]
            if pending is not None:
                w_ref[g - 1] = _sb_exp(*pending, mask)
            pending = (log_beta, survive, carries[g])
            totals.append(total)
        w_ref[hp - 1] = _sb_exp(*pending, mask)
        return totals

    acc_ref[...] = jnp.zeros_like(acc_ref)
    for g, raw in enumerate(_sb_raw(q_ref, lanes, kv_blocks(k_ref, qi))):
        raw_ref[g] = raw
    carries = visit(causal, None, jnp.maximum(qi - 1, 0), [0.0] * hp)
    meta_logs = _sb_logs(_sb_raw(q_ref, lanes, [km_ref[:, ln] for ln in lanes]), meta_valid)

    def body(t, carries):
        block = qi - t
        totals = visit(None, block + 1, jnp.maximum(block - 1, 0), carries)
        return tuple(c + tot for c, tot in zip(carries, totals))

    carries = lax.fori_loop(1, qi + 1, body, tuple(carries))

    accumulate_pv(0)
    log_betas, splits, _ = meta_logs
    weights = _sb_weights(log_betas, splits, _suffix_sum_matrix(n_pad), carries, meta_valid)
    for ln, a in zip(lanes, weights):
        o_ref[:, ln] = (acc_ref[:, ln] + _dot(a, vm_ref[:, ln])).astype(o_ref.dtype)


def _attention(q, kv, kv_meta, *, batch, seq, n_meta, tq, hp):
    m, d = q.shape
    width = hp * HEAD_DIM
    groups = d // width
    nq = seq // tq
    n_pad = kv_meta.shape[0]
    vmem = _vmem_budget(pipelined=2 * (tq + seq + n_pad) * width * 2,
                        resident=tq * width * 4 + 2 * tq * tq * 2 + hp * tq * tq * (4 + 2),
                        temps=hp * 16 * tq * tq * 4)
    return pl.pallas_call(
        functools.partial(_attn_kernel, tq=tq, hp=hp, n_meta=n_meta),
        grid=(batch, groups, nq),
        in_specs=[
            pl.BlockSpec((tq, width), lambda b, h, i: (b * nq + i, h)),
            pl.BlockSpec((seq, width), lambda b, h, i: (b, h)),
            pl.BlockSpec((seq, width), lambda b, h, i: (b, groups + h)),
            pl.BlockSpec((n_pad, width), lambda b, h, i: (0, h)),
            pl.BlockSpec((n_pad, width), lambda b, h, i: (0, groups + h)),
        ],
        out_specs=pl.BlockSpec((tq, width), lambda b, h, i: (b * nq + i, h)),
        out_shape=jax.ShapeDtypeStruct((m, d), BF16),
        scratch_shapes=[pltpu.VMEM((tq, width), F32), pltpu.VMEM((2 * tq, tq), BF16),
                        pltpu.VMEM((hp, tq, tq), F32), pltpu.VMEM((hp, tq, tq), BF16)],
        compiler_params=_compiler_params(("parallel", "parallel", "arbitrary"), vmem),
        name="sb_attention",
    )(q, kv, kv, kv_meta, kv_meta)


def _conv_halo(v_main, v_meta, *, batch, seq, tm):
    d = v_main.shape[1]
    nt = seq // tm
    tails = v_main.reshape(batch, nt, tm, d)[:, :nt - 1, tm - 2:, :]
    first = jnp.broadcast_to(v_meta[None, None, -2:, :], (batch, 1, 2, d))
    prev = jnp.concatenate([first, tails], axis=1).reshape(batch * nt, 2, d)
    return jnp.pad(prev, ((0, 0), (HALO_ROWS - 2, 0), (0, 0)))


def kernel(x, meta_tokens, ln_gain, ln_bias, ffn_w_in, ffn_w_out, conv_w_in, conv_w, conv_b,
           conv_w_out, sb_w_q, sb_w_kv, sb_w_o):
    batch, seq, d = x.shape
    n_meta = meta_tokens.shape[0]
    depth = ffn_w_in.shape[0]
    n_conv = conv_w_in.shape[0]
    alpha = float((2 * depth) ** 0.25)
    assert d % HEAD_DIM == 0 and n_meta % SUBLANES_BF16 == 0 and n_meta <= LANES

    tm = 512
    tf = 512
    tm_proj = 1024
    tn_proj = d
    tn_conv = 512
    tq = 256
    hp = 8

    w_in = ffn_w_in.astype(BF16)
    w_out = ffn_w_out.astype(BF16)
    cw_in = conv_w_in.astype(BF16)
    cw_out = conv_w_out.astype(BF16)
    w_q = sb_w_q.astype(BF16)
    w_kv = sb_w_kv.astype(BF16)[None]
    w_o = sb_w_o.astype(BF16)
    gain = ln_gain.reshape(depth, 3, 1, d)
    bias = ln_bias.reshape(depth, 3, 1, d)
    cb = conv_b.reshape(n_conv, 1, d)

    h = x.reshape(batch * seq, d)
    hb = h.astype(BF16)
    hm = meta_tokens.astype(x.dtype)
    hmb = hm.astype(BF16)
    kv = kv_meta = None
    for i in range(depth):
        meta_live = i < n_conv
        h, hb = _ffn_ln(h, hb, w_in, w_out, gain, bias, i, 0, 0, alpha=alpha, tm=tm, tf=tf)
        if meta_live:
            hm, hmb = _ffn_ln(hm, hmb, w_in, w_out, gain, bias, i, 0, 0, alpha=alpha, tm=n_meta,
                              tf=tf)
        if i < n_conv:
            bg, v = _conv_in(hb, cw_in, i, tm=tm_proj, tn=tn_conv)
            bgm, vm = _conv_in(hmb, cw_in, i, tm=n_meta, tn=tn_conv)
            halo = _conv_halo(v, vm, batch=batch, seq=seq, tm=tm)
            h, hb = _conv_out_ln(bg, v, halo, conv_w, cb, cw_out, h, gain, bias, i, i, 1,
                                 alpha=alpha, tm=tm)
            halo_m = jnp.zeros((1, HALO_ROWS, d), BF16)
            hm, hmb = _conv_out_ln(bgm, vm, halo_m, conv_w, cb, cw_out, hm, gain, bias, i, i, 1,
                                   alpha=alpha, tm=n_meta)
        else:
            j = i - n_conv
            q = _proj(hb, w_q, j, tm=tm_proj, tn=tn_proj, out_scale=HEAD_DIM ** -0.5)
            o = _attention(q, kv, kv_meta, batch=batch, seq=seq, n_meta=n_meta, tq=tq, hp=hp)
            h, hb = _mix_out_ln(o, w_o, h, gain, bias, j, i, 1, alpha=alpha, tm=tm)
        h, hb = _ffn_ln(h, hb, w_in, w_out, gain, bias, i, 1, 2, alpha=alpha, tm=tm, tf=tf)
        if meta_live:
            hm, hmb = _ffn_ln(hm, hmb, w_in, w_out, gain, bias, i, 1, 2, alpha=alpha, tm=n_meta,
                              tf=tf)
        if i == n_conv - 1:
            kv = _proj(hb, w_kv, 0, tm=tm_proj, tn=tn_proj)
            kvm = _proj(hmb, w_kv, 0, tm=n_meta, tn=tn_proj)
            kv_meta = jnp.pad(kvm, ((0, LANES - n_meta), (0, 0)))
    return h.reshape(batch, seq, d)
```

```python
import functools

import jax
import jax.numpy as jnp
from jax import lax
from jax.experimental import pallas as pl
from jax.experimental.pallas import tpu as pltpu

LN_EPS = 1e-5
HEAD_DIM = 128
LANES = 128
SUBLANES_BF16 = 16
HALO_ROWS = 8
MXU_ROWS_PER_WEIGHT_TILE = 256
VMEM_LIMIT_CAP = 60000 * 1024
COMPILER_SCRATCH_BYTES = 4 << 20

LOG2E = 1.4426950408889634

BF16 = jnp.bfloat16
F32 = jnp.float32


def _vmem_budget(*, pipelined, resident, temps):
    return 2 * pipelined + resident + temps + COMPILER_SCRATCH_BYTES


def _compiler_params(semantics, vmem_bytes):
    return pltpu.CompilerParams(
        dimension_semantics=semantics,
        vmem_limit_bytes=int(min(VMEM_LIMIT_CAP, vmem_bytes)),
    )


def _layer_norm(y, gain, bias):
    mu = jnp.mean(y, axis=-1, keepdims=True)
    yc = y - mu
    var = jnp.mean(yc * yc, axis=-1, keepdims=True)
    return yc * lax.rsqrt(var + LN_EPS) * gain + bias


def _dot(a, b):
    return jnp.dot(a, b, preferred_element_type=F32)


def _ffn_kernel(x_ref, wg_ref, wu_ref, wo_ref, gb_ref, o_ref, ob_ref, xb_ref, *, alpha, nf):
    f = pl.program_id(1)
    tm = x_ref.shape[0]

    def branch(chunks):
        gates = [_dot(xb_ref[rows, :], wg_ref[...]) for rows in chunks]
        ups = [_dot(xb_ref[rows, :], wu_ref[...]) for rows in chunks]
        acts = [(g * jax.nn.sigmoid(g) * u).astype(BF16) for g, u in zip(gates, ups)]
        return [_dot(act, wo_ref[...]) for act in acts]

    @pl.when(f == 0)
    def _():
        xb_ref[...] = x_ref[...].astype(BF16)
        o_ref[...] = (2.0 * alpha) * x_ref[...] + branch([slice(0, tm)])[0]

    @pl.when(jnp.logical_and(f > 0, f < nf - 1))
    def _():
        o_ref[...] += branch([slice(0, tm)])[0]

    @pl.when(f == nf - 1)
    def _():
        chunks = _row_chunks(tm)
        for rows, prod in zip(chunks, branch(chunks)):
            out = _layer_norm(0.5 * (o_ref[rows, :] + prod), gb_ref[0:1, :], gb_ref[1:2, :])
            o_ref[rows, :] = out
            ob_ref[rows, :] = out.astype(BF16)


def _ffn_ln(h, w_in, w_out, gain_bias, li, lj, lk, *, alpha, tm, tf):
    m, d = h.shape
    ff = w_out.shape[2]
    nf = ff // tf
    assert nf >= 2
    vmem = _vmem_budget(pipelined=2 * tm * d * 4 + tm * d * 2 + 3 * d * tf * 2 + 2 * d * 4,
                        resident=tm * d * 2, temps=5 * tm * tf * 4 + 3 * tm * d * 4)
    return pl.pallas_call(
        functools.partial(_ffn_kernel, alpha=alpha, nf=nf),
        grid=(m // tm, nf),
        in_specs=[
            pl.BlockSpec((tm, d), lambda i, f: (i, 0)),
            pl.BlockSpec((None, None, d, tf), lambda i, f: (li, lj, 0, f)),
            pl.BlockSpec((None, None, d, tf), lambda i, f: (li, lj, 0, nf + f)),
            pl.BlockSpec((None, None, tf, d), lambda i, f: (li, lj, f, 0)),
            pl.BlockSpec((None, None, 2, d), lambda i, f: (li, lk, 0, 0)),
        ],
        out_specs=[pl.BlockSpec((tm, d), lambda i, f: (i, 0))] * 2,
        out_shape=[jax.ShapeDtypeStruct((m, d), F32), jax.ShapeDtypeStruct((m, d), BF16)],
        scratch_shapes=[pltpu.VMEM((tm, d), BF16)],
        compiler_params=_compiler_params(("parallel", "arbitrary"), vmem),
        name="ffn_ln",
    )(h, w_in, w_in, w_out, gain_bias)


def _proj_kernel(xb_ref, w_ref, o_ref, *, out_scale):
    y = _dot(xb_ref[...], w_ref[...])
    if out_scale != 1.0:
        y = y * out_scale
    o_ref[...] = y.astype(o_ref.dtype)


def _proj(hb, w, li, *, tm, tn, out_scale=1.0):
    m, d = hb.shape
    n = w.shape[2]
    vmem = _vmem_budget(pipelined=tm * d * 2 + d * tn * 2 + tm * tn * 2,
                        resident=0, temps=2 * tm * tn * 4)
    return pl.pallas_call(
        functools.partial(_proj_kernel, out_scale=out_scale),
        grid=(m // tm, n // tn),
        in_specs=[
            pl.BlockSpec((tm, d), lambda i, j: (i, 0)),
            pl.BlockSpec((None, d, tn), lambda i, j: (li, 0, j)),
        ],
        out_specs=pl.BlockSpec((tm, tn), lambda i, j: (i, j)),
        out_shape=jax.ShapeDtypeStruct((m, n), BF16),
        compiler_params=_compiler_params(("parallel", "arbitrary"), vmem),
        name="proj",
    )(hb, w)


def _conv_in_kernel(xb_ref, wb_ref, wc_ref, wu_ref, b_ref, v_ref):
    xb = xb_ref[...]
    b_ref[...] = _dot(xb, wb_ref[...]).astype(BF16)
    v_ref[...] = (_dot(xb, wc_ref[...]) * _dot(xb, wu_ref[...])).astype(BF16)


def _conv_in(hb, w_in, li, *, tm, tn):
    m, d = hb.shape
    nd = d // tn
    vmem = _vmem_budget(pipelined=tm * d * 2 + 3 * d * tn * 2 + 2 * tm * tn * 2,
                        resident=0, temps=4 * tm * tn * 4)
    return pl.pallas_call(
        _conv_in_kernel,
        grid=(m // tm, nd),
        in_specs=[
            pl.BlockSpec((tm, d), lambda i, j: (i, 0)),
            pl.BlockSpec((None, d, tn), lambda i, j: (li, 0, j)),
            pl.BlockSpec((None, d, tn), lambda i, j: (li, 0, nd + j)),
            pl.BlockSpec((None, d, tn), lambda i, j: (li, 0, 2 * nd + j)),
        ],
        out_specs=[
            pl.BlockSpec((tm, tn), lambda i, j: (i, j)),
            pl.BlockSpec((tm, tn), lambda i, j: (i, j)),
        ],
        out_shape=[jax.ShapeDtypeStruct((m, d), BF16), jax.ShapeDtypeStruct((m, d), BF16)],
        compiler_params=_compiler_params(("parallel", "arbitrary"), vmem),
        name="conv_in",
    )(hb, w_in, w_in, w_in)


def _row_chunks(tm):
    rc = min(tm, MXU_ROWS_PER_WEIGHT_TILE)
    return [slice(r, r + rc) for r in range(0, tm, rc)]


def _residual_ln_chunks(chunks, products, x_ref, g_ref, b_ref, o_ref, alpha):
    for rows, prod in zip(chunks, products):
        o_ref[rows, :] = _layer_norm(alpha * x_ref[rows, :] + prod, g_ref[...], b_ref[...])


def _mix_out_kernel(lhs_ref, w_ref, x_ref, g_ref, b_ref, o_ref, *, alpha):
    chunks = _row_chunks(x_ref.shape[0])
    products = [_dot(lhs_ref[rows, :], w_ref[...]) for rows in chunks]
    _residual_ln_chunks(chunks, products, x_ref, g_ref, b_ref, o_ref, alpha)


def _conv_out_kernel(bg_ref, v_ref, halo_ref, cw_ref, cb_ref, w_ref, x_ref, g_ref, b_ref,
                     o_ref, *, alpha):
    chunks = _row_chunks(x_ref.shape[0])
    cw = cw_ref[...]
    products = []
    for rows in chunks:
        v0 = v_ref[rows, :].astype(F32)
        if rows.start == 0:
            before = halo_ref[...].astype(F32)
        else:
            before = v_ref[rows.start - SUBLANES_BF16:rows.start, :].astype(F32)
        prev1 = before[-1:]
        prev2 = before[-2:-1]
        row = lax.broadcasted_iota(jnp.int32, (v0.shape[0], 1), 0)
        v1 = jnp.where(row == 0, prev1, pltpu.roll(v0, 1, axis=0))
        v2 = jnp.where(row == 0, prev2, jnp.where(row == 1, prev1, pltpu.roll(v0, 2, axis=0)))
        conv = v2 * cw[0:1] + v1 * cw[1:2] + v0 * cw[2:3] + cb_ref[...]
        lhs = (bg_ref[rows, :].astype(F32) * conv).astype(BF16)
        products.append(_dot(lhs, w_ref[...]))
    _residual_ln_chunks(chunks, products, x_ref, g_ref, b_ref, o_ref, alpha)


def _ln_out(m, d, tm):
    return dict(out_specs=pl.BlockSpec((tm, d), lambda i: (i, 0)),
                out_shape=jax.ShapeDtypeStruct((m, d), F32))


def _mix_out_specs(tm, d, wi, li, lk):
    return [
        pl.BlockSpec((None, d, d), lambda i: (wi, 0, 0)),
        pl.BlockSpec((tm, d), lambda i: (i, 0)),
        pl.BlockSpec((None, None, 1, d), lambda i: (li, lk, 0, 0)),
        pl.BlockSpec((None, None, 1, d), lambda i: (li, lk, 0, 0)),
    ]


def _mix_out_ln(lhs, w, h, gain, bias, wi, li, lk, *, alpha, tm):
    m, d = h.shape
    vmem = _vmem_budget(pipelined=tm * d * 2 + d * d * 2 + 2 * tm * d * 4 + 2 * d * 4,
                        resident=0, temps=4 * tm * d * 4)
    return pl.pallas_call(
        functools.partial(_mix_out_kernel, alpha=alpha),
        grid=(m // tm,),
        in_specs=[pl.BlockSpec((tm, d), lambda i: (i, 0))] + _mix_out_specs(tm, d, wi, li, lk),
        **_ln_out(m, d, tm),
        compiler_params=_compiler_params(("parallel",), vmem),
        name="mix_out_ln",
    )(lhs, w, h, gain, bias)


def _conv_out_ln(bg, v, halo, cw, cb, w, h, gain, bias, ci, li, lk, *, alpha, tm):
    m, d = h.shape
    taps = cw.shape[1]
    vmem = _vmem_budget(pipelined=2 * tm * d * 2 + d * d * 2 + 2 * tm * d * 4
                        + (HALO_ROWS + taps + 3) * d * 4, resident=0, temps=8 * tm * d * 4)
    return pl.pallas_call(
        functools.partial(_conv_out_kernel, alpha=alpha),
        grid=(m // tm,),
        in_specs=[
            pl.BlockSpec((tm, d), lambda i: (i, 0)),
            pl.BlockSpec((tm, d), lambda i: (i, 0)),
            pl.BlockSpec((None, HALO_ROWS, d), lambda i: (i, 0, 0)),
            pl.BlockSpec((None, taps, d), lambda i: (ci, 0, 0)),
            pl.BlockSpec((None, 1, d), lambda i: (ci, 0, 0)),
        ] + _mix_out_specs(tm, d, ci, li, lk),
        **_ln_out(m, d, tm),
        compiler_params=_compiler_params(("parallel",), vmem),
        name="conv_out_ln",
    )(bg, v, halo, cw, cb, w, h, gain, bias)


def _suffix_sum_matrix(n):
    r = lax.broadcasted_iota(jnp.int32, (2 * n, n), 0)
    c = lax.broadcasted_iota(jnp.int32, (2 * n, n), 1)
    return (jnp.where(r >= n, r - n, r) > c).astype(BF16)


def _sb_raw(q_ref, lanes, k_blks):
    return [lax.dot_general(q_ref[:, ln], k, (((1,), (1,)), ((), ())), preferred_element_type=F32)
            for ln, k in zip(lanes, k_blks)]


def _sb_logs(zs, mask):
    log_betas, splits, totals = [], [], []
    for z in zs:
        soft = jnp.log(1.0 + jnp.exp2(jnp.abs(z) * -LOG2E))
        log_beta = jnp.minimum(z, 0.0) - soft
        log_keep = log_beta - z
        if mask is not None:
            log_keep = jnp.where(mask, log_keep, 0.0)
        hi = log_keep.astype(BF16)
        lo = (log_keep - hi.astype(F32)).astype(BF16)
        log_betas.append(log_beta)
        splits.append(jnp.concatenate([hi, lo], axis=1))
        totals.append(jnp.sum(log_keep, axis=-1, keepdims=True))
    return log_betas, splits, totals


def _sb_exp(log_beta, survive, carry, mask):
    a = jnp.exp(log_beta + survive + carry)
    if mask is not None:
        a = jnp.where(mask, a, 0.0)
    return a.astype(BF16)


def _sb_weights(log_betas, splits, suffix, carries, mask):
    survives = [_dot(split, suffix[...]) for split in splits]
    return [_sb_exp(lb, sv, c, mask) for lb, sv, c in zip(log_betas, survives, carries)]


def _attn_kernel(q_ref, k_ref, v_ref, km_ref, vm_ref, o_ref, acc_ref, suffix_ref, raw_ref, w_ref,
                 *, tq, hp, n_meta):
    qi = pl.program_id(2)
    suffix_ref[...] = _suffix_sum_matrix(tq)
    row = lax.broadcasted_iota(jnp.int32, (tq, tq), 0)
    col = lax.broadcasted_iota(jnp.int32, (tq, tq), 1)
    causal = col < row
    n_pad = km_ref.shape[0]
    meta_valid = lax.broadcasted_iota(jnp.int32, (tq, n_pad), 1) < n_meta
    lanes = [slice(g * HEAD_DIM, (g + 1) * HEAD_DIM) for g in range(hp)]

    def kv_blocks(ref, block):
        start = pl.multiple_of(block * tq, tq)
        return [ref[pl.ds(start, tq), ln] for ln in lanes]

    def accumulate_pv(block):
        for g, (ln, v) in enumerate(zip(lanes, kv_blocks(v_ref, block))):
            acc_ref[:, ln] += _dot(w_ref[g], v)

    def visit(mask, pv_block, next_block, carries, with_meta=False):
        next_k = kv_blocks(k_ref, next_block)
        prev_v = None if pv_block is None else kv_blocks(v_ref, pv_block)
        totals, meta_logs, pending, pending_meta = [], [], None, None
        for g, ln in enumerate(lanes):
            (log_beta,), (split,), (total,) = _sb_logs([raw_ref[g]], mask)
            survive = _dot(split, suffix_ref[...])
            if prev_v is not None:
                acc_ref[:, ln] += _dot(w_ref[g], prev_v[g])
            raw_ref[g] = _sb_raw(q_ref, [ln], [next_k[g]])[0]
            meta_z = _sb_raw(q_ref, [ln], [km_ref[:, ln]]) if with_meta else None
            if pending is not None:
                w_ref[g - 1] = _sb_exp(*pending, mask)
                if with_meta:
                    meta_logs.append(_sb_logs(pending_meta, meta_valid))
            pending, pending_meta = (log_beta, survive, carries[g]), meta_z
            totals.append(total)
        w_ref[hp - 1] = _sb_exp(*pending, mask)
        if with_meta:
            meta_logs.append(_sb_logs(pending_meta, meta_valid))
        return totals, meta_logs

    acc_ref[...] = jnp.zeros_like(acc_ref)
    for g, raw in enumerate(_sb_raw(q_ref, lanes, kv_blocks(k_ref, qi))):
        raw_ref[g] = raw
    carries, meta_logs = visit(causal, None, jnp.maximum(qi - 1, 0), [0.0] * hp, with_meta=True)

    def body(t, carries):
        block = qi - t
        totals, _ = visit(None, block + 1, jnp.maximum(block - 1, 0), carries)
        return tuple(c + tot for c, tot in zip(carries, totals))

    carries = lax.fori_loop(1, qi + 1, body, tuple(carries))

    accumulate_pv(0)
    log_betas = [logs[0][0] for logs in meta_logs]
    splits = [logs[1][0] for logs in meta_logs]
    weights = _sb_weights(log_betas, splits, _suffix_sum_matrix(n_pad), carries, meta_valid)
    for ln, a in zip(lanes, weights):
        o_ref[:, ln] = (acc_ref[:, ln] + _dot(a, vm_ref[:, ln])).astype(o_ref.dtype)


def _attention(q, kv, kv_meta, *, batch, seq, n_meta, tq, hp):
    m, d = q.shape
    width = hp * HEAD_DIM
    groups = d // width
    nq = seq // tq
    n_pad = kv_meta.shape[0]
    vmem = _vmem_budget(pipelined=2 * (tq + seq + n_pad) * width * 2,
                        resident=tq * width * 4 + 2 * tq * tq * 2 + hp * tq * tq * (4 + 2),
                        temps=hp * 16 * tq * tq * 4)
    return pl.pallas_call(
        functools.partial(_attn_kernel, tq=tq, hp=hp, n_meta=n_meta),
        grid=(batch, groups, nq),
        in_specs=[
            pl.BlockSpec((tq, width), lambda b, h, i: (b * nq + i, h)),
            pl.BlockSpec((seq, width), lambda b, h, i: (b, h)),
            pl.BlockSpec((seq, width), lambda b, h, i: (b, groups + h)),
            pl.BlockSpec((n_pad, width), lambda b, h, i: (0, h)),
            pl.BlockSpec((n_pad, width), lambda b, h, i: (0, groups + h)),
        ],
        out_specs=pl.BlockSpec((tq, width), lambda b, h, i: (b * nq + i, h)),
        out_shape=jax.ShapeDtypeStruct((m, d), BF16),
        scratch_shapes=[pltpu.VMEM((tq, width), F32), pltpu.VMEM((2 * tq, tq), BF16),
                        pltpu.VMEM((hp, tq, tq), F32), pltpu.VMEM((hp, tq, tq), BF16)],
        compiler_params=_compiler_params(("parallel", "parallel", "arbitrary"), vmem),
        name="sb_attention",
    )(q, kv, kv, kv_meta, kv_meta)


def _conv_halo(v_main, v_meta, *, batch, seq, tm):
    d = v_main.shape[1]
    nt = seq // tm
    tails = v_main.reshape(batch, nt, tm, d)[:, :nt - 1, tm - 2:, :]
    first = jnp.broadcast_to(v_meta[None, None, -2:, :], (batch, 1, 2, d))
    prev = jnp.concatenate([first, tails], axis=1).reshape(batch * nt, 2, d)
    return jnp.pad(prev, ((0, 0), (HALO_ROWS - 2, 0), (0, 0)))


def kernel(x, meta_tokens, ln_gain, ln_bias, ffn_w_in, ffn_w_out, conv_w_in, conv_w, conv_b,
           conv_w_out, sb_w_q, sb_w_kv, sb_w_o):
    batch, seq, d = x.shape
    n_meta = meta_tokens.shape[0]
    depth = ffn_w_in.shape[0]
    n_conv = conv_w_in.shape[0]
    alpha = float((2 * depth) ** 0.25)
    assert d % HEAD_DIM == 0 and n_meta % SUBLANES_BF16 == 0 and n_meta <= LANES

    tm = 512
    tf = 512
    tm_proj = 1024
    tn_proj = d
    tn_conv = 512
    tq = 256
    hp = 8

    w_in = ffn_w_in.astype(BF16)
    w_out = ffn_w_out.astype(BF16)
    cw_in = conv_w_in.astype(BF16)
    cw_out = conv_w_out.astype(BF16)
    w_q = sb_w_q.astype(BF16)
    w_kv = sb_w_kv.astype(BF16)[None]
    w_o = sb_w_o.astype(BF16)
    gain_bias = jnp.stack([ln_gain, ln_bias], axis=2)
    gain = ln_gain.reshape(depth, 3, 1, d)
    bias = ln_bias.reshape(depth, 3, 1, d)
    cb = conv_b.reshape(n_conv, 1, d)

    h = x.reshape(batch * seq, d)
    hm = meta_tokens.astype(x.dtype)
    kv = kv_meta = None
    for i in range(depth):
        meta_live = i < n_conv
        h, hb = _ffn_ln(h, w_in, w_out, gain_bias, i, 0, 0, alpha=alpha, tm=tm, tf=tf)
        if meta_live:
            hm, hmb = _ffn_ln(hm, w_in, w_out, gain_bias, i, 0, 0, alpha=alpha, tm=n_meta, tf=tf)
        if i < n_conv:
            bg, v = _conv_in(hb, cw_in, i, tm=tm_proj, tn=tn_conv)
            bgm, vm = _conv_in(hmb, cw_in, i, tm=n_meta, tn=tn_conv)
            halo = _conv_halo(v, vm, batch=batch, seq=seq, tm=tm)
            h = _conv_out_ln(bg, v, halo, conv_w, cb, cw_out, h, gain, bias, i, i, 1,
                             alpha=alpha, tm=tm)
            halo_m = jnp.zeros((1, HALO_ROWS, d), BF16)
            hm = _conv_out_ln(bgm, vm, halo_m, conv_w, cb, cw_out, hm, gain, bias, i, i, 1,
                              alpha=alpha, tm=n_meta)
        else:
            j = i - n_conv
            q = _proj(hb, w_q, j, tm=tm_proj, tn=tn_proj, out_scale=HEAD_DIM ** -0.5)
            o = _attention(q, kv, kv_meta, batch=batch, seq=seq, n_meta=n_meta, tq=tq, hp=hp)
            h = _mix_out_ln(o, w_o, h, gain, bias, j, i, 1, alpha=alpha, tm=tm)
        h, hb = _ffn_ln(h, w_in, w_out, gain_bias, i, 1, 2, alpha=alpha, tm=tm, tf=tf)
        if meta_live:
            hm, hmb = _ffn_ln(hm, w_in, w_out, gain_bias, i, 1, 2, alpha=alpha, tm=n_meta, tf=tf)
        if i == n_conv - 1:
            kv = _proj(hb, w_kv, 0, tm=tm_proj, tn=tn_proj)
            kvm = _proj(hmb, w_kv, 0, tm=n_meta, tn=tn_proj)
            kv_meta = jnp.pad(kvm, ((0, LANES - n_meta), (0, 0)))
    return h.reshape(batch, seq, d)
```

```python
import functools

import jax
import jax.numpy as jnp
from jax import lax
from jax.experimental import pallas as pl
from jax.experimental.pallas import tpu as pltpu

LN_EPS = 1e-5
HEAD_DIM = 128
LANES = 128
SUBLANES_BF16 = 16
HALO_ROWS = 8
MXU_ROWS_PER_WEIGHT_TILE = 256
VMEM_LIMIT_CAP = 60000 * 1024
COMPILER_SCRATCH_BYTES = 4 << 20

LOG2E = 1.4426950408889634

BF16 = jnp.bfloat16
F32 = jnp.float32


def _vmem_budget(*, pipelined, resident, temps):
    return 2 * pipelined + resident + temps + COMPILER_SCRATCH_BYTES


def _compiler_params(semantics, vmem_bytes):
    return pltpu.CompilerParams(
        dimension_semantics=semantics,
        vmem_limit_bytes=int(min(VMEM_LIMIT_CAP, vmem_bytes)),
    )


def _layer_norm(y, gain, bias):
    mu = jnp.mean(y, axis=-1, keepdims=True)
    yc = y - mu
    var = jnp.mean(yc * yc, axis=-1, keepdims=True)
    return yc * lax.rsqrt(var + LN_EPS) * gain + bias


def _dot(a, b):
    return jnp.dot(a, b, preferred_element_type=F32)


def _ffn_kernel(x_ref, wg_a, wu_a, wo_a, wg_b, wu_b, wo_b, gb_ref, o_ref, ob_ref, xb_ref, *,
                alpha, ns, last_is_pair):
    s = pl.program_id(1)
    tm = x_ref.shape[0]
    pair = [(wg_a, wu_a, wo_a), (wg_b, wu_b, wo_b)]

    def branch(chunks, weights):
        units = [(rows, w) for rows in chunks for w in weights]
        gates = [_dot(xb_ref[rows, :], wg[...]) for rows, (wg, _, _) in units]
        ups = [_dot(xb_ref[rows, :], wu[...]) for rows, (_, wu, _) in units]
        acts = [(g * jax.nn.sigmoid(g) * u).astype(BF16) for g, u in zip(gates, ups)]
        prods = [_dot(act, wo[...]) for act, (_, (_, _, wo)) in zip(acts, units)]
        n = len(weights)
        return [prods[c * n:(c + 1) * n] for c in range(len(chunks))]

    @pl.when(s == 0)
    def _():
        xb_ref[...] = x_ref[...].astype(BF16)
        p_a, p_b = branch([slice(0, tm)], pair)[0]
        o_ref[...] = (2.0 * alpha) * x_ref[...] + p_a
        o_ref[...] += p_b

    @pl.when(jnp.logical_and(s > 0, s < ns - 1))
    def _():
        p_a, p_b = branch([slice(0, tm)], pair)[0]
        o_ref[...] += p_a
        o_ref[...] += p_b

    @pl.when(s == ns - 1)
    def _():
        chunks = _row_chunks(tm)
        for rows, prods in zip(chunks, branch(chunks, pair if last_is_pair else pair[:1])):
            out = _layer_norm(0.5 * (o_ref[rows, :] + sum(prods)), gb_ref[0:1, :], gb_ref[1:2, :])
            o_ref[rows, :] = out
            ob_ref[rows, :] = out.astype(BF16)


def _ffn_ln(h, w_in, w_out, gain_bias, li, lj, lk, *, alpha, tm, tf):
    m, d = h.shape
    ff = w_out.shape[2]
    nf = ff // tf
    ns = pl.cdiv(nf, 2)
    assert ns >= 2
    last_is_pair = nf % 2 == 0
    b_max = nf - 1 if last_is_pair else nf - 2

    def chunk_a(s):
        return 2 * s

    def chunk_b(s):
        return jnp.minimum(2 * s + 1, b_max)

    def weight_specs(chunk):
        return [
            pl.BlockSpec((None, None, d, tf), lambda i, s: (li, lj, 0, chunk(s))),
            pl.BlockSpec((None, None, d, tf), lambda i, s: (li, lj, 0, nf + chunk(s))),
            pl.BlockSpec((None, None, tf, d), lambda i, s: (li, lj, chunk(s), 0)),
        ]

    vmem = _vmem_budget(pipelined=2 * tm * d * 4 + tm * d * 2 + 6 * d * tf * 2 + 2 * d * 4,
                        resident=tm * d * 2,
                        temps=4 * tm * tf * 4 + 2 * tm * tf * 2 + tm * d * 4
                        + 3 * MXU_ROWS_PER_WEIGHT_TILE * d * 4)
    return pl.pallas_call(
        functools.partial(_ffn_kernel, alpha=alpha, ns=ns, last_is_pair=last_is_pair),
        grid=(m // tm, ns),
        in_specs=[pl.BlockSpec((tm, d), lambda i, s: (i, 0))]
        + weight_specs(chunk_a) + weight_specs(chunk_b)
        + [pl.BlockSpec((None, None, 2, d), lambda i, s: (li, lk, 0, 0))],
        out_specs=[pl.BlockSpec((tm, d), lambda i, s: (i, 0))] * 2,
        out_shape=[jax.ShapeDtypeStruct((m, d), F32), jax.ShapeDtypeStruct((m, d), BF16)],
        scratch_shapes=[pltpu.VMEM((tm, d), BF16)],
        compiler_params=_compiler_params(("parallel", "arbitrary"), vmem),
        name="ffn_ln",
    )(h, w_in, w_in, w_out, w_in, w_in, w_out, gain_bias)


def _proj_kernel(xb_ref, w_ref, o_ref, *, out_scale):
    y = _dot(xb_ref[...], w_ref[...])
    if out_scale != 1.0:
        y = y * out_scale
    o_ref[...] = y.astype(o_ref.dtype)


def _proj(hb, w, li, *, tm, tn, out_scale=1.0):
    m, d = hb.shape
    n = w.shape[2]
    vmem = _vmem_budget(pipelined=tm * d * 2 + d * tn * 2 + tm * tn * 2,
                        resident=0, temps=2 * tm * tn * 4)
    return pl.pallas_call(
        functools.partial(_proj_kernel, out_scale=out_scale),
        grid=(m // tm, n // tn),
        in_specs=[
            pl.BlockSpec((tm, d), lambda i, j: (i, 0)),
            pl.BlockSpec((None, d, tn), lambda i, j: (li, 0, j)),
        ],
        out_specs=pl.BlockSpec((tm, tn), lambda i, j: (i, j)),
        out_shape=jax.ShapeDtypeStruct((m, n), BF16),
        compiler_params=_compiler_params(("parallel", "arbitrary"), vmem),
        name="proj",
    )(hb, w)


def _conv_in_kernel(xb_ref, wb_ref, wc_ref, wu_ref, b_ref, v_ref):
    xb = xb_ref[...]
    b_ref[...] = _dot(xb, wb_ref[...]).astype(BF16)
    v_ref[...] = (_dot(xb, wc_ref[...]) * _dot(xb, wu_ref[...])).astype(BF16)


def _conv_in(hb, w_in, li, *, tm, tn):
    m, d = hb.shape
    nd = d // tn
    vmem = _vmem_budget(pipelined=tm * d * 2 + 3 * d * tn * 2 + 2 * tm * tn * 2,
                        resident=0, temps=4 * tm * tn * 4)
    return pl.pallas_call(
        _conv_in_kernel,
        grid=(m // tm, nd),
        in_specs=[
            pl.BlockSpec((tm, d), lambda i, j: (i, 0)),
            pl.BlockSpec((None, d, tn), lambda i, j: (li, 0, j)),
            pl.BlockSpec((None, d, tn), lambda i, j: (li, 0, nd + j)),
            pl.BlockSpec((None, d, tn), lambda i, j: (li, 0, 2 * nd + j)),
        ],
        out_specs=[
            pl.BlockSpec((tm, tn), lambda i, j: (i, j)),
            pl.BlockSpec((tm, tn), lambda i, j: (i, j)),
        ],
        out_shape=[jax.ShapeDtypeStruct((m, d), BF16), jax.ShapeDtypeStruct((m, d), BF16)],
        compiler_params=_compiler_params(("parallel", "arbitrary"), vmem),
        name="conv_in",
    )(hb, w_in, w_in, w_in)


def _row_chunks(tm):
    rc = min(tm, MXU_ROWS_PER_WEIGHT_TILE)
    return [slice(r, r + rc) for r in range(0, tm, rc)]


def _residual_ln_chunks(chunks, products, x_ref, g_ref, b_ref, o_ref, alpha):
    for rows, prod in zip(chunks, products):
        o_ref[rows, :] = _layer_norm(alpha * x_ref[rows, :] + prod, g_ref[...], b_ref[...])


def _mix_out_kernel(lhs_ref, w_ref, x_ref, g_ref, b_ref, o_ref, *, alpha):
    chunks = _row_chunks(x_ref.shape[0])
    products = [_dot(lhs_ref[rows, :], w_ref[...]) for rows in chunks]
    _residual_ln_chunks(chunks, products, x_ref, g_ref, b_ref, o_ref, alpha)


def _conv_out_kernel(bg_ref, v_ref, halo_ref, cw_ref, cb_ref, w_ref, x_ref, g_ref, b_ref,
                     o_ref, *, alpha):
    chunks = _row_chunks(x_ref.shape[0])
    cw = cw_ref[...]
    products = []
    for rows in chunks:
        v0 = v_ref[rows, :].astype(F32)
        if rows.start == 0:
            before = halo_ref[...].astype(F32)
        else:
            before = v_ref[rows.start - SUBLANES_BF16:rows.start, :].astype(F32)
        prev1 = before[-1:]
        prev2 = before[-2:-1]
        row = lax.broadcasted_iota(jnp.int32, (v0.shape[0], 1), 0)
        v1 = jnp.where(row == 0, prev1, pltpu.roll(v0, 1, axis=0))
        v2 = jnp.where(row == 0, prev2, jnp.where(row == 1, prev1, pltpu.roll(v0, 2, axis=0)))
        conv = v2 * cw[0:1] + v1 * cw[1:2] + v0 * cw[2:3] + cb_ref[...]
        lhs = (bg_ref[rows, :].astype(F32) * conv).astype(BF16)
        products.append(_dot(lhs, w_ref[...]))
    _residual_ln_chunks(chunks, products, x_ref, g_ref, b_ref, o_ref, alpha)


def _ln_out(m, d, tm):
    return dict(out_specs=pl.BlockSpec((tm, d), lambda i: (i, 0)),
                out_shape=jax.ShapeDtypeStruct((m, d), F32))


def _mix_out_specs(tm, d, wi, li, lk):
    return [
        pl.BlockSpec((None, d, d), lambda i: (wi, 0, 0)),
        pl.BlockSpec((tm, d), lambda i: (i, 0)),
        pl.BlockSpec((None, None, 1, d), lambda i: (li, lk, 0, 0)),
        pl.BlockSpec((None, None, 1, d), lambda i: (li, lk, 0, 0)),
    ]


def _mix_out_ln(lhs, w, h, gain, bias, wi, li, lk, *, alpha, tm):
    m, d = h.shape
    vmem = _vmem_budget(pipelined=tm * d * 2 + d * d * 2 + 2 * tm * d * 4 + 2 * d * 4,
                        resident=0, temps=4 * tm * d * 4)
    return pl.pallas_call(
        functools.partial(_mix_out_kernel, alpha=alpha),
        grid=(m // tm,),
        in_specs=[pl.BlockSpec((tm, d), lambda i: (i, 0))] + _mix_out_specs(tm, d, wi, li, lk),
        **_ln_out(m, d, tm),
        compiler_params=_compiler_params(("parallel",), vmem),
        name="mix_out_ln",
    )(lhs, w, h, gain, bias)


def _conv_out_ln(bg, v, halo, cw, cb, w, h, gain, bias, ci, li, lk, *, alpha, tm):
    m, d = h.shape
    taps = cw.shape[1]
    vmem = _vmem_budget(pipelined=2 * tm * d * 2 + d * d * 2 + 2 * tm * d * 4
                        + (HALO_ROWS + taps + 3) * d * 4, resident=0, temps=8 * tm * d * 4)
    return pl.pallas_call(
        functools.partial(_conv_out_kernel, alpha=alpha),
        grid=(m // tm,),
        in_specs=[
            pl.BlockSpec((tm, d), lambda i: (i, 0)),
            pl.BlockSpec((tm, d), lambda i: (i, 0)),
            pl.BlockSpec((None, HALO_ROWS, d), lambda i: (i, 0, 0)),
            pl.BlockSpec((None, taps, d), lambda i: (ci, 0, 0)),
            pl.BlockSpec((None, 1, d), lambda i: (ci, 0, 0)),
        ] + _mix_out_specs(tm, d, ci, li, lk),
        **_ln_out(m, d, tm),
        compiler_params=_compiler_params(("parallel",), vmem),
        name="conv_out_ln",
    )(bg, v, halo, cw, cb, w, h, gain, bias)


def _suffix_sum_matrix(n):
    r = lax.broadcasted_iota(jnp.int32, (2 * n, n), 0)
    c = lax.broadcasted_iota(jnp.int32, (2 * n, n), 1)
    return (jnp.where(r >= n, r - n, r) > c).astype(BF16)


def _sb_raw(q_ref, lanes, k_blks):
    return [lax.dot_general(q_ref[:, ln], k, (((1,), (1,)), ((), ())), preferred_element_type=F32)
            for ln, k in zip(lanes, k_blks)]


def _sb_logs(zs, mask):
    log_betas, splits, totals = [], [], []
    for z in zs:
        soft = jnp.log(1.0 + jnp.exp2(jnp.abs(z) * -LOG2E))
        log_beta = jnp.minimum(z, 0.0) - soft
        log_keep = log_beta - z
        if mask is not None:
            log_keep = jnp.where(mask, log_keep, 0.0)
        hi = log_keep.astype(BF16)
        lo = (log_keep - hi.astype(F32)).astype(BF16)
        log_betas.append(log_beta)
        splits.append(jnp.concatenate([hi, lo], axis=1))
        totals.append(jnp.sum(log_keep, axis=-1, keepdims=True))
    return log_betas, splits, totals


def _sb_exp(log_beta, survive, carry, mask):
    a = jnp.exp(log_beta + survive + carry)
    if mask is not None:
        a = jnp.where(mask, a, 0.0)
    return a.astype(BF16)


def _sb_weights(log_betas, splits, suffix, carries, mask):
    survives = [_dot(split, suffix[...]) for split in splits]
    return [_sb_exp(lb, sv, c, mask) for lb, sv, c in zip(log_betas, survives, carries)]


def _attn_kernel(q_ref, k_ref, v_ref, km_ref, vm_ref, o_ref, acc_ref, suffix_ref, raw_ref, w_ref,
                 *, tq, hp, n_meta):
    qi = pl.program_id(2)
    suffix_ref[...] = _suffix_sum_matrix(tq)
    row = lax.broadcasted_iota(jnp.int32, (tq, tq), 0)
    col = lax.broadcasted_iota(jnp.int32, (tq, tq), 1)
    causal = col < row
    n_pad = km_ref.shape[0]
    meta_valid = lax.broadcasted_iota(jnp.int32, (tq, n_pad), 1) < n_meta
    lanes = [slice(g * HEAD_DIM, (g + 1) * HEAD_DIM) for g in range(hp)]

    def kv_blocks(ref, block):
        start = pl.multiple_of(block * tq, tq)
        return [ref[pl.ds(start, tq), ln] for ln in lanes]

    def accumulate_pv(block):
        for g, (ln, v) in enumerate(zip(lanes, kv_blocks(v_ref, block))):
            acc_ref[:, ln] += _dot(w_ref[g], v)

    def visit(mask, pv_block, next_block, carries, with_meta=False):
        next_k = kv_blocks(k_ref, next_block)
        prev_v = None if pv_block is None else kv_blocks(v_ref, pv_block)
        totals, meta_logs, pending, pending_meta = [], [], None, None
        for g, ln in enumerate(lanes):
            (log_beta,), (split,), (total,) = _sb_logs([raw_ref[g]], mask)
            survive = _dot(split, suffix_ref[...])
            if prev_v is not None:
                acc_ref[:, ln] += _dot(w_ref[g], prev_v[g])
            raw_ref[g] = _sb_raw(q_ref, [ln], [next_k[g]])[0]
            meta_z = _sb_raw(q_ref, [ln], [km_ref[:, ln]]) if with_meta else None
            if pending is not None:
                w_ref[g - 1] = _sb_exp(*pending, mask)
                if with_meta:
                    meta_logs.append(_sb_logs(pending_meta, meta_valid))
            pending, pending_meta = (log_beta, survive, carries[g]), meta_z
            totals.append(total)
        w_ref[hp - 1] = _sb_exp(*pending, mask)
        if with_meta:
            meta_logs.append(_sb_logs(pending_meta, meta_valid))
        return totals, meta_logs

    acc_ref[...] = jnp.zeros_like(acc_ref)
    for g, raw in enumerate(_sb_raw(q_ref, lanes, kv_blocks(k_ref, qi))):
        raw_ref[g] = raw
    carries, meta_logs = visit(causal, None, jnp.maximum(qi - 1, 0), [0.0] * hp, with_meta=True)

    def body(t, carries):
        block = qi - t
        totals, _ = visit(None, block + 1, jnp.maximum(block - 1, 0), carries)
        return tuple(c + tot for c, tot in zip(carries, totals))

    carries = lax.fori_loop(1, qi + 1, body, tuple(carries))

    accumulate_pv(0)
    log_betas = [logs[0][0] for logs in meta_logs]
    splits = [logs[1][0] for logs in meta_logs]
    weights = _sb_weights(log_betas, splits, _suffix_sum_matrix(n_pad), carries, meta_valid)
    for ln, a in zip(lanes, weights):
        o_ref[:, ln] = (acc_ref[:, ln] + _dot(a, vm_ref[:, ln])).astype(o_ref.dtype)


def _attention(q, kv, kv_meta, *, batch, seq, n_meta, tq, hp):
    m, d = q.shape
    width = hp * HEAD_DIM
    groups = d // width
    nq = seq // tq
    n_pad = kv_meta.shape[0]
    vmem = _vmem_budget(pipelined=2 * (tq + seq + n_pad) * width * 2,
                        resident=tq * width * 4 + 2 * tq * tq * 2 + hp * tq * tq * (4 + 2),
                        temps=hp * 16 * tq * tq * 4)
    return pl.pallas_call(
        functools.partial(_attn_kernel, tq=tq, hp=hp, n_meta=n_meta),
        grid=(batch, groups, nq),
        in_specs=[
            pl.BlockSpec((tq, width), lambda b, h, i: (b * nq + i, h)),
            pl.BlockSpec((seq, width), lambda b, h, i: (b, h)),
            pl.BlockSpec((seq, width), lambda b, h, i: (b, groups + h)),
            pl.BlockSpec((n_pad, width), lambda b, h, i: (0, h)),
            pl.BlockSpec((n_pad, width), lambda b, h, i: (0, groups + h)),
        ],
        out_specs=pl.BlockSpec((tq, width), lambda b, h, i: (b * nq + i, h)),
        out_shape=jax.ShapeDtypeStruct((m, d), BF16),
        scratch_shapes=[pltpu.VMEM((tq, width), F32), pltpu.VMEM((2 * tq, tq), BF16),
                        pltpu.VMEM((hp, tq, tq), F32), pltpu.VMEM((hp, tq, tq), BF16)],
        compiler_params=_compiler_params(("parallel", "parallel", "arbitrary"), vmem),
        name="sb_attention",
    )(q, kv, kv, kv_meta, kv_meta)


def _conv_halo(v_main, v_meta, *, batch, seq, tm):
    d = v_main.shape[1]
    nt = seq // tm
    tails = v_main.reshape(batch, nt, tm, d)[:, :nt - 1, tm - 2:, :]
    first = jnp.broadcast_to(v_meta[None, None, -2:, :], (batch, 1, 2, d))
    prev = jnp.concatenate([first, tails], axis=1).reshape(batch * nt, 2, d)
    return jnp.pad(prev, ((0, 0), (HALO_ROWS - 2, 0), (0, 0)))


def kernel(x, meta_tokens, ln_gain, ln_bias, ffn_w_in, ffn_w_out, conv_w_in, conv_w, conv_b,
           conv_w_out, sb_w_q, sb_w_kv, sb_w_o):
    batch, seq, d = x.shape
    n_meta = meta_tokens.shape[0]
    depth = ffn_w_in.shape[0]
    n_conv = conv_w_in.shape[0]
    alpha = float((2 * depth) ** 0.25)
    assert d % HEAD_DIM == 0 and n_meta % SUBLANES_BF16 == 0 and n_meta <= LANES

    tm = 512
    tf = 512
    tm_proj = 1024
    tn_proj = d
    tn_conv = 512
    tq = 256
    hp = 8

    w_in = ffn_w_in.astype(BF16)
    w_out = ffn_w_out.astype(BF16)
    cw_in = conv_w_in.astype(BF16)
    cw_out = conv_w_out.astype(BF16)
    w_q = sb_w_q.astype(BF16)
    w_kv = sb_w_kv.astype(BF16)[None]
    w_o = sb_w_o.astype(BF16)
    gain_bias = jnp.stack([ln_gain, ln_bias], axis=2)
    gain = ln_gain.reshape(depth, 3, 1, d)
    bias = ln_bias.reshape(depth, 3, 1, d)
    cb = conv_b.reshape(n_conv, 1, d)

    h = x.reshape(batch * seq, d)
    hm = meta_tokens.astype(x.dtype)
    kv = kv_meta = None
    for i in range(depth):
        meta_live = i < n_conv
        h, hb = _ffn_ln(h, w_in, w_out, gain_bias, i, 0, 0, alpha=alpha, tm=tm, tf=tf)
        if meta_live:
            hm, hmb = _ffn_ln(hm, w_in, w_out, gain_bias, i, 0, 0, alpha=alpha, tm=n_meta, tf=tf)
        if i < n_conv:
            bg, v = _conv_in(hb, cw_in, i, tm=tm_proj, tn=tn_conv)
            bgm, vm = _conv_in(hmb, cw_in, i, tm=n_meta, tn=tn_conv)
            halo = _conv_halo(v, vm, batch=batch, seq=seq, tm=tm)
            h = _conv_out_ln(bg, v, halo, conv_w, cb, cw_out, h, gain, bias, i, i, 1,
                             alpha=alpha, tm=tm)
            halo_m = jnp.zeros((1, HALO_ROWS, d), BF16)
            hm = _conv_out_ln(bgm, vm, halo_m, conv_w, cb, cw_out, hm, gain, bias, i, i, 1,
                              alpha=alpha, tm=n_meta)
        else:
            j = i - n_conv
            q = _proj(hb, w_q, j, tm=tm_proj, tn=tn_proj, out_scale=HEAD_DIM ** -0.5)
            o = _attention(q, kv, kv_meta, batch=batch, seq=seq, n_meta=n_meta, tq=tq, hp=hp)
            h = _mix_out_ln(o, w_o, h, gain, bias, j, i, 1, alpha=alpha, tm=tm)
        h, hb = _ffn_ln(h, w_in, w_out, gain_bias, i, 1, 2, alpha=alpha, tm=tm, tf=tf)
        if meta_live:
            hm, hmb = _ffn_ln(hm, w_in, w_out, gain_bias, i, 1, 2, alpha=alpha, tm=n_meta, tf=tf)
        if i == n_conv - 1:
            kv = _proj(hb, w_kv, 0, tm=tm_proj, tn=tn_proj)
            kvm = _proj(hmb, w_kv, 0, tm=n_meta, tn=tn_proj)
            kv_meta = jnp.pad(kvm, ((0, LANES - n_meta), (0, 0)))
    return h.reshape(batch, seq, d)
```

```python
import functools

import jax
import jax.numpy as jnp
from jax import lax
from jax.experimental import pallas as pl
from jax.experimental.pallas import tpu as pltpu

LN_EPS = 1e-5
HEAD_DIM = 128
LANES = 128
SUBLANES_BF16 = 16
HALO_ROWS = 8
MXU_ROWS_PER_WEIGHT_TILE = 256
CONVERT_COLUMN_SPLITS = 4
VMEM_LIMIT_CAP = 60000 * 1024
COMPILER_SCRATCH_BYTES = 4 << 20

LOG2E = 1.4426950408889634

BF16 = jnp.bfloat16
F32 = jnp.float32


def _vmem_budget(*, pipelined, resident, temps):
    return 2 * pipelined + resident + temps + COMPILER_SCRATCH_BYTES


def _compiler_params(semantics, vmem_bytes):
    return pltpu.CompilerParams(
        dimension_semantics=semantics,
        vmem_limit_bytes=int(min(VMEM_LIMIT_CAP, vmem_bytes)),
    )


def _layer_norm(y, gain, bias):
    mu = jnp.mean(y, axis=-1, keepdims=True)
    yc = y - mu
    var = jnp.mean(yc * yc, axis=-1, keepdims=True)
    return yc * lax.rsqrt(var + LN_EPS) * gain + bias


def _dot(a, b):
    return jnp.dot(a, b, preferred_element_type=F32)


def _ffn_kernel(*refs, alpha, ns, last_is_pair, convert):
    x_ref, wg_a, wu_a, wo_a, wg_b, wu_b, wo_b, gb_ref = refs[:8]
    rest = refs[8:]
    if convert:
        (next_in, next_out), rest = rest[:2], rest[2:]
    (o_ref, ob_ref), rest = rest[:2], rest[2:]
    if convert:
        (next_in_bf, next_out_bf), rest = rest[:2], rest[2:]
    (xb_ref,) = rest
    s = pl.program_id(1)

    def convert_slab():
        if convert:
            next_in_bf[...] = next_in[...].astype(BF16)
            next_out_bf[...] = next_out[...].astype(BF16)

    tm = x_ref.shape[0]
    pair = [(wg_a, wu_a, wo_a), (wg_b, wu_b, wo_b)]

    def branch(chunks, weights):
        units = [(rows, w) for rows in chunks for w in weights]
        gates = [_dot(xb_ref[rows, :], wg[...]) for rows, (wg, _, _) in units]
        ups = [_dot(xb_ref[rows, :], wu[...]) for rows, (_, wu, _) in units]
        acts = [(g * jax.nn.sigmoid(g) * u).astype(BF16) for g, u in zip(gates, ups)]
        prods = [_dot(act, wo[...]) for act, (_, (_, _, wo)) in zip(acts, units)]
        n = len(weights)
        return [prods[c * n:(c + 1) * n] for c in range(len(chunks))]

    @pl.when(s == 0)
    def _():
        xb_ref[...] = x_ref[...].astype(BF16)
        p_a, p_b = branch([slice(0, tm)], pair)[0]
        convert_slab()
        o_ref[...] = (2.0 * alpha) * x_ref[...] + p_a
        o_ref[...] += p_b

    @pl.when(jnp.logical_and(s > 0, s < ns - 1))
    def _():
        p_a, p_b = branch([slice(0, tm)], pair)[0]
        convert_slab()
        o_ref[...] += p_a
        o_ref[...] += p_b

    @pl.when(s == ns - 1)
    def _():
        convert_slab()
        chunks = _row_chunks(tm)
        for rows, prods in zip(chunks, branch(chunks, pair if last_is_pair else pair[:1])):
            out = _layer_norm(0.5 * (o_ref[rows, :] + sum(prods)), gb_ref[0:1, :], gb_ref[1:2, :])
            o_ref[rows, :] = out
            ob_ref[rows, :] = out.astype(BF16)


def _ffn_ln(h, w_in, w_out, gain_bias, li, lk, *, alpha, tm, tf, next_f32=None):
    m, d = h.shape
    ff = w_out.shape[0]
    nf = ff // tf
    ns = pl.cdiv(nf, 2)
    nt = m // tm
    assert ns >= 2
    last_is_pair = nf % 2 == 0
    b_max = nf - 1 if last_is_pair else nf - 2

    def chunk_a(s):
        return 2 * s

    def chunk_b(s):
        return jnp.minimum(2 * s + 1, b_max)

    def weight_specs(chunk):
        return [
            pl.BlockSpec((d, tf), lambda i, s: (0, chunk(s))),
            pl.BlockSpec((d, tf), lambda i, s: (0, nf + chunk(s))),
            pl.BlockSpec((tf, d), lambda i, s: (chunk(s), 0)),
        ]

    in_specs = ([pl.BlockSpec((tm, d), lambda i, s: (i, 0))]
                + weight_specs(chunk_a) + weight_specs(chunk_b)
                + [pl.BlockSpec((None, None, 2, d), lambda i, s: (li, lk, 0, 0))])
    out_specs = [pl.BlockSpec((tm, d), lambda i, s: (i, 0))] * 2
    out_shape = [jax.ShapeDtypeStruct((m, d), F32), jax.ShapeDtypeStruct((m, d), BF16)]
    operands = [h, w_in, w_in, w_out, w_in, w_in, w_out, gain_bias]
    slab_bytes = 0
    if next_f32 is not None:
        next_in, next_out, ni, nj = next_f32
        for w_next in (next_in, next_out):
            rows, cols = w_next.shape[2] // nt, w_next.shape[3] // CONVERT_COLUMN_SPLITS
            assert rows * nt == w_next.shape[2] and rows % SUBLANES_BF16 == 0
            assert cols * CONVERT_COLUMN_SPLITS == w_next.shape[3] and cols % LANES == 0
            in_specs.append(pl.BlockSpec(
                (None, None, rows, cols),
                lambda i, s: (ni, nj, i, jnp.minimum(s, CONVERT_COLUMN_SPLITS - 1))))
            out_specs.append(pl.BlockSpec(
                (rows, cols), lambda i, s: (i, jnp.minimum(s, CONVERT_COLUMN_SPLITS - 1))))
            out_shape.append(jax.ShapeDtypeStruct(w_next.shape[2:], BF16))
            operands.append(w_next)
            slab_bytes += rows * cols * (4 + 2)
        assert ns >= CONVERT_COLUMN_SPLITS

    vmem = _vmem_budget(pipelined=2 * tm * d * 4 + tm * d * 2 + 6 * d * tf * 2 + 2 * d * 4
                        + slab_bytes, resident=tm * d * 2,
                        temps=4 * tm * tf * 4 + 2 * tm * tf * 2 + tm * d * 4
                        + 3 * MXU_ROWS_PER_WEIGHT_TILE * d * 4)
    return pl.pallas_call(
        functools.partial(_ffn_kernel, alpha=alpha, ns=ns, last_is_pair=last_is_pair,
                          convert=next_f32 is not None),
        grid=(nt, ns),
        in_specs=in_specs,
        out_specs=out_specs,
        out_shape=out_shape,
        scratch_shapes=[pltpu.VMEM((tm, d), BF16)],
        compiler_params=_compiler_params(("parallel", "arbitrary"), vmem),
        name="ffn_ln",
    )(*operands)


def _proj_kernel(xb_ref, w_ref, o_ref, *, out_scale):
    y = _dot(xb_ref[...], w_ref[...])
    if out_scale != 1.0:
        y = y * out_scale
    o_ref[...] = y.astype(o_ref.dtype)


def _proj(hb, w, li, *, tm, tn, out_scale=1.0):
    m, d = hb.shape
    n = w.shape[2]
    vmem = _vmem_budget(pipelined=tm * d * 2 + d * tn * 2 + tm * tn * 2,
                        resident=0, temps=2 * tm * tn * 4)
    return pl.pallas_call(
        functools.partial(_proj_kernel, out_scale=out_scale),
        grid=(m // tm, n // tn),
        in_specs=[
            pl.BlockSpec((tm, d), lambda i, j: (i, 0)),
            pl.BlockSpec((None, d, tn), lambda i, j: (li, 0, j)),
        ],
        out_specs=pl.BlockSpec((tm, tn), lambda i, j: (i, j)),
        out_shape=jax.ShapeDtypeStruct((m, n), BF16),
        compiler_params=_compiler_params(("parallel", "arbitrary"), vmem),
        name="proj",
    )(hb, w)


def _conv_in_kernel(xb_ref, wb_ref, wc_ref, wu_ref, b_ref, v_ref):
    xb = xb_ref[...]
    b_ref[...] = _dot(xb, wb_ref[...]).astype(BF16)
    v_ref[...] = (_dot(xb, wc_ref[...]) * _dot(xb, wu_ref[...])).astype(BF16)


def _conv_in(hb, w_in, li, *, tm, tn):
    m, d = hb.shape
    nd = d // tn
    vmem = _vmem_budget(pipelined=tm * d * 2 + 3 * d * tn * 2 + 2 * tm * tn * 2,
                        resident=0, temps=4 * tm * tn * 4)
    return pl.pallas_call(
        _conv_in_kernel,
        grid=(m // tm, nd),
        in_specs=[
            pl.BlockSpec((tm, d), lambda i, j: (i, 0)),
            pl.BlockSpec((None, d, tn), lambda i, j: (li, 0, j)),
            pl.BlockSpec((None, d, tn), lambda i, j: (li, 0, nd + j)),
            pl.BlockSpec((None, d, tn), lambda i, j: (li, 0, 2 * nd + j)),
        ],
        out_specs=[
            pl.BlockSpec((tm, tn), lambda i, j: (i, j)),
            pl.BlockSpec((tm, tn), lambda i, j: (i, j)),
        ],
        out_shape=[jax.ShapeDtypeStruct((m, d), BF16), jax.ShapeDtypeStruct((m, d), BF16)],
        compiler_params=_compiler_params(("parallel", "arbitrary"), vmem),
        name="conv_in",
    )(hb, w_in, w_in, w_in)


def _row_chunks(tm):
    rc = min(tm, MXU_ROWS_PER_WEIGHT_TILE)
    return [slice(r, r + rc) for r in range(0, tm, rc)]


def _residual_ln_chunks(chunks, products, x_ref, g_ref, b_ref, o_ref, alpha):
    for rows, prod in zip(chunks, products):
        o_ref[rows, :] = _layer_norm(alpha * x_ref[rows, :] + prod, g_ref[...], b_ref[...])


def _mix_out_kernel(lhs_ref, w_ref, x_ref, g_ref, b_ref, o_ref, *, alpha):
    chunks = _row_chunks(x_ref.shape[0])
    products = [_dot(lhs_ref[rows, :], w_ref[...]) for rows in chunks]
    _residual_ln_chunks(chunks, products, x_ref, g_ref, b_ref, o_ref, alpha)


def _conv_out_kernel(bg_ref, v_ref, halo_ref, cw_ref, cb_ref, w_ref, x_ref, g_ref, b_ref,
                     o_ref, *, alpha):
    chunks = _row_chunks(x_ref.shape[0])
    cw = cw_ref[...]
    products = []
    for rows in chunks:
        v0 = v_ref[rows, :].astype(F32)
        if rows.start == 0:
            before = halo_ref[...].astype(F32)
        else:
            before = v_ref[rows.start - SUBLANES_BF16:rows.start, :].astype(F32)
        prev1 = before[-1:]
        prev2 = before[-2:-1]
        row = lax.broadcasted_iota(jnp.int32, (v0.shape[0], 1), 0)
        v1 = jnp.where(row == 0, prev1, pltpu.roll(v0, 1, axis=0))
        v2 = jnp.where(row == 0, prev2, jnp.where(row == 1, prev1, pltpu.roll(v0, 2, axis=0)))
        conv = v2 * cw[0:1] + v1 * cw[1:2] + v0 * cw[2:3] + cb_ref[...]
        lhs = (bg_ref[rows, :].astype(F32) * conv).astype(BF16)
        products.append(_dot(lhs, w_ref[...]))
    _residual_ln_chunks(chunks, products, x_ref, g_ref, b_ref, o_ref, alpha)


def _ln_out(m, d, tm):
    return dict(out_specs=pl.BlockSpec((tm, d), lambda i: (i, 0)),
                out_shape=jax.ShapeDtypeStruct((m, d), F32))


def _mix_out_specs(tm, d, wi, li, lk):
    return [
        pl.BlockSpec((None, d, d), lambda i: (wi, 0, 0)),
        pl.BlockSpec((tm, d), lambda i: (i, 0)),
        pl.BlockSpec((None, None, 1, d), lambda i: (li, lk, 0, 0)),
        pl.BlockSpec((None, None, 1, d), lambda i: (li, lk, 0, 0)),
    ]


def _mix_out_ln(lhs, w, h, gain, bias, wi, li, lk, *, alpha, tm):
    m, d = h.shape
    vmem = _vmem_budget(pipelined=tm * d * 2 + d * d * 2 + 2 * tm * d * 4 + 2 * d * 4,
                        resident=0, temps=4 * tm * d * 4)
    return pl.pallas_call(
        functools.partial(_mix_out_kernel, alpha=alpha),
        grid=(m // tm,),
        in_specs=[pl.BlockSpec((tm, d), lambda i: (i, 0))] + _mix_out_specs(tm, d, wi, li, lk),
        **_ln_out(m, d, tm),
        compiler_params=_compiler_params(("parallel",), vmem),
        name="mix_out_ln",
    )(lhs, w, h, gain, bias)


def _conv_out_ln(bg, v, halo, cw, cb, w, h, gain, bias, ci, li, lk, *, alpha, tm):
    m, d = h.shape
    taps = cw.shape[1]
    vmem = _vmem_budget(pipelined=2 * tm * d * 2 + d * d * 2 + 2 * tm * d * 4
                        + (HALO_ROWS + taps + 3) * d * 4, resident=0, temps=8 * tm * d * 4)
    return pl.pallas_call(
        functools.partial(_conv_out_kernel, alpha=alpha),
        grid=(m // tm,),
        in_specs=[
            pl.BlockSpec((tm, d), lambda i: (i, 0)),
            pl.BlockSpec((tm, d), lambda i: (i, 0)),
            pl.BlockSpec((None, HALO_ROWS, d), lambda i: (i, 0, 0)),
            pl.BlockSpec((None, taps, d), lambda i: (ci, 0, 0)),
            pl.BlockSpec((None, 1, d), lambda i: (ci, 0, 0)),
        ] + _mix_out_specs(tm, d, ci, li, lk),
        **_ln_out(m, d, tm),
        compiler_params=_compiler_params(("parallel",), vmem),
        name="conv_out_ln",
    )(bg, v, halo, cw, cb, w, h, gain, bias)


def _suffix_sum_matrix(n):
    r = lax.broadcasted_iota(jnp.int32, (2 * n, n), 0)
    c = lax.broadcasted_iota(jnp.int32, (2 * n, n), 1)
    return (jnp.where(r >= n, r - n, r) > c).astype(BF16)


def _sb_raw(q_ref, lanes, k_blks):
    return [lax.dot_general(q_ref[:, ln], k, (((1,), (1,)), ((), ())), preferred_element_type=F32)
            for ln, k in zip(lanes, k_blks)]


def _sb_logs(zs, mask):
    log_betas, splits, totals = [], [], []
    for z in zs:
        soft = jnp.log(1.0 + jnp.exp2(jnp.abs(z) * -LOG2E))
        log_beta = jnp.minimum(z, 0.0) - soft
        log_keep = log_beta - z
        if mask is not None:
            log_keep = jnp.where(mask, log_keep, 0.0)
        hi = log_keep.astype(BF16)
        lo = (log_keep - hi.astype(F32)).astype(BF16)
        log_betas.append(log_beta)
        splits.append(jnp.concatenate([hi, lo], axis=1))
        totals.append(jnp.sum(log_keep, axis=-1, keepdims=True))
    return log_betas, splits, totals


def _sb_exp(log_beta, survive, carry, mask):
    a = jnp.exp(log_beta + survive + carry)
    if mask is not None:
        a = jnp.where(mask, a, 0.0)
    return a.astype(BF16)


def _sb_weights(log_betas, splits, suffix, carries, mask):
    survives = [_dot(split, suffix[...]) for split in splits]
    return [_sb_exp(lb, sv, c, mask) for lb, sv, c in zip(log_betas, survives, carries)]


def _attn_kernel(q_ref, k_ref, v_ref, km_ref, vm_ref, o_ref, acc_ref, suffix_ref, raw_ref, w_ref,
                 *, tq, hp, n_meta):
    qi = pl.program_id(2)
    suffix_ref[...] = _suffix_sum_matrix(tq)
    row = lax.broadcasted_iota(jnp.int32, (tq, tq), 0)
    col = lax.broadcasted_iota(jnp.int32, (tq, tq), 1)
    causal = col < row
    n_pad = km_ref.shape[0]
    meta_valid = lax.broadcasted_iota(jnp.int32, (tq, n_pad), 1) < n_meta
    lanes = [slice(g * HEAD_DIM, (g + 1) * HEAD_DIM) for g in range(hp)]

    def kv_blocks(ref, block):
        start = pl.multiple_of(block * tq, tq)
        return [ref[pl.ds(start, tq), ln] for ln in lanes]

    def accumulate_pv(block):
        for g, (ln, v) in enumerate(zip(lanes, kv_blocks(v_ref, block))):
            acc_ref[:, ln] += _dot(w_ref[g], v)

    def visit(mask, pv_block, next_block, carries, with_meta=False):
        next_k = kv_blocks(k_ref, next_block)
        prev_v = None if pv_block is None else kv_blocks(v_ref, pv_block)
        totals, meta_logs, pending, pending_meta = [], [], None, None
        for g, ln in enumerate(lanes):
            (log_beta,), (split,), (total,) = _sb_logs([raw_ref[g]], mask)
            survive = _dot(split, suffix_ref[...])
            if prev_v is not None:
                acc_ref[:, ln] += _dot(w_ref[g], prev_v[g])
            raw_ref[g] = _sb_raw(q_ref, [ln], [next_k[g]])[0]
            meta_z = _sb_raw(q_ref, [ln], [km_ref[:, ln]]) if with_meta else None
            if pending is not None:
                w_ref[g - 1] = _sb_exp(*pending, mask)
                if with_meta:
                    meta_logs.append(_sb_logs(pending_meta, meta_valid))
            pending, pending_meta = (log_beta, survive, carries[g]), meta_z
            totals.append(total)
        w_ref[hp - 1] = _sb_exp(*pending, mask)
        if with_meta:
            meta_logs.append(_sb_logs(pending_meta, meta_valid))
        return totals, meta_logs

    acc_ref[...] = jnp.zeros_like(acc_ref)
    for g, raw in enumerate(_sb_raw(q_ref, lanes, kv_blocks(k_ref, qi))):
        raw_ref[g] = raw
    carries, meta_logs = visit(causal, None, jnp.maximum(qi - 1, 0), [0.0] * hp, with_meta=True)

    def body(t, carries):
        block = qi - t
        totals, _ = visit(None, block + 1, jnp.maximum(block - 1, 0), carries)
        return tuple(c + tot for c, tot in zip(carries, totals))

    carries = lax.fori_loop(1, qi + 1, body, tuple(carries))

    accumulate_pv(0)
    log_betas = [logs[0][0] for logs in meta_logs]
    splits = [logs[1][0] for logs in meta_logs]
    weights = _sb_weights(log_betas, splits, _suffix_sum_matrix(n_pad), carries, meta_valid)
    for ln, a in zip(lanes, weights):
        o_ref[:, ln] = (acc_ref[:, ln] + _dot(a, vm_ref[:, ln])).astype(o_ref.dtype)


def _attention(q, kv, kv_meta, *, batch, seq, n_meta, tq, hp):
    m, d = q.shape
    width = hp * HEAD_DIM
    groups = d // width
    nq = seq // tq
    n_pad = kv_meta.shape[0]
    vmem = _vmem_budget(pipelined=2 * (tq + seq + n_pad) * width * 2,
                        resident=tq * width * 4 + 2 * tq * tq * 2 + hp * tq * tq * (4 + 2),
                        temps=hp * 16 * tq * tq * 4)
    return pl.pallas_call(
        functools.partial(_attn_kernel, tq=tq, hp=hp, n_meta=n_meta),
        grid=(batch, groups, nq),
        in_specs=[
            pl.BlockSpec((tq, width), lambda b, h, i: (b * nq + i, h)),
            pl.BlockSpec((seq, width), lambda b, h, i: (b, h)),
            pl.BlockSpec((seq, width), lambda b, h, i: (b, groups + h)),
            pl.BlockSpec((n_pad, width), lambda b, h, i: (0, h)),
            pl.BlockSpec((n_pad, width), lambda b, h, i: (0, groups + h)),
        ],
        out_specs=pl.BlockSpec((tq, width), lambda b, h, i: (b * nq + i, h)),
        out_shape=jax.ShapeDtypeStruct((m, d), BF16),
        scratch_shapes=[pltpu.VMEM((tq, width), F32), pltpu.VMEM((2 * tq, tq), BF16),
                        pltpu.VMEM((hp, tq, tq), F32), pltpu.VMEM((hp, tq, tq), BF16)],
        compiler_params=_compiler_params(("parallel", "parallel", "arbitrary"), vmem),
        name="sb_attention",
    )(q, kv, kv, kv_meta, kv_meta)


def _conv_halo(v_main, v_meta, *, batch, seq, tm):
    d = v_main.shape[1]
    nt = seq // tm
    tails = v_main.reshape(batch, nt, tm, d)[:, :nt - 1, tm - 2:, :]
    first = jnp.broadcast_to(v_meta[None, None, -2:, :], (batch, 1, 2, d))
    prev = jnp.concatenate([first, tails], axis=1).reshape(batch * nt, 2, d)
    return jnp.pad(prev, ((0, 0), (HALO_ROWS - 2, 0), (0, 0)))


def kernel(x, meta_tokens, ln_gain, ln_bias, ffn_w_in, ffn_w_out, conv_w_in, conv_w, conv_b,
           conv_w_out, sb_w_q, sb_w_kv, sb_w_o):
    batch, seq, d = x.shape
    n_meta = meta_tokens.shape[0]
    depth = ffn_w_in.shape[0]
    n_conv = conv_w_in.shape[0]
    alpha = float((2 * depth) ** 0.25)
    assert d % HEAD_DIM == 0 and n_meta % SUBLANES_BF16 == 0 and n_meta <= LANES

    tm = 512
    tf = 512
    tm_proj = 1024
    tn_proj = d
    tn_conv = 512
    tq = 256
    hp = 8

    cw_in = conv_w_in.astype(BF16)
    cw_out = conv_w_out.astype(BF16)
    w_q = sb_w_q.astype(BF16)
    w_kv = sb_w_kv.astype(BF16)[None]
    w_o = sb_w_o.astype(BF16)
    gain_bias = jnp.stack([ln_gain, ln_bias], axis=2)
    gain = ln_gain.reshape(depth, 3, 1, d)
    bias = ln_bias.reshape(depth, 3, 1, d)
    cb = conv_b.reshape(n_conv, 1, d)

    ffn_order = [(i, j) for i in range(depth) for j in range(2)]
    w_in = ffn_w_in[0, 0].astype(BF16)
    w_out = ffn_w_out[0, 0].astype(BF16)

    def ffn_pair(h, hm, w_in, w_out, i, j):
        k = ffn_order.index((i, j))
        lk = 2 * j
        if k + 1 < len(ffn_order):
            h, hb, next_w_in, next_w_out = _ffn_ln(
                h, w_in, w_out, gain_bias, i, lk, alpha=alpha, tm=tm, tf=tf,
                next_f32=(ffn_w_in, ffn_w_out) + ffn_order[k + 1])
        else:
            h, hb = _ffn_ln(h, w_in, w_out, gain_bias, i, lk, alpha=alpha, tm=tm, tf=tf)
            next_w_in = next_w_out = None
        hmb = None
        if i < n_conv:
            hm, hmb = _ffn_ln(hm, w_in, w_out, gain_bias, i, lk, alpha=alpha, tm=n_meta, tf=tf)
        return h, hb, hm, hmb, next_w_in, next_w_out

    h = x.reshape(batch * seq, d)
    hm = meta_tokens.astype(x.dtype)
    kv = kv_meta = None
    for i in range(depth):
        h, hb, hm, hmb, w_in, w_out = ffn_pair(h, hm, w_in, w_out, i, 0)
        if i < n_conv:
            bg, v = _conv_in(hb, cw_in, i, tm=tm_proj, tn=tn_conv)
            bgm, vm = _conv_in(hmb, cw_in, i, tm=n_meta, tn=tn_conv)
            halo = _conv_halo(v, vm, batch=batch, seq=seq, tm=tm)
            h = _conv_out_ln(bg, v, halo, conv_w, cb, cw_out, h, gain, bias, i, i, 1,
                             alpha=alpha, tm=tm)
            halo_m = jnp.zeros((1, HALO_ROWS, d), BF16)
            hm = _conv_out_ln(bgm, vm, halo_m, conv_w, cb, cw_out, hm, gain, bias, i, i, 1,
                              alpha=alpha, tm=n_meta)
        else:
            j = i - n_conv
            q = _proj(hb, w_q, j, tm=tm_proj, tn=tn_proj, out_scale=HEAD_DIM ** -0.5)
            o = _attention(q, kv, kv_meta, batch=batch, seq=seq, n_meta=n_meta, tq=tq, hp=hp)
            h = _mix_out_ln(o, w_o, h, gain, bias, j, i, 1, alpha=alpha, tm=tm)
        h, hb, hm, hmb, w_in, w_out = ffn_pair(h, hm, w_in, w_out, i, 1)
        if i == n_conv - 1:
            kv = _proj(hb, w_kv, 0, tm=tm_proj, tn=tn_proj)
            kvm = _proj(hmb, w_kv, 0, tm=n_meta, tn=tn_proj)
            kv_meta = jnp.pad(kvm, ((0, LANES - n_meta), (0, 0)))
    return h.reshape(batch, seq, d)
```

```python
import functools

import jax
import jax.numpy as jnp
from jax import lax
from jax.experimental import pallas as pl
from jax.experimental.pallas import tpu as pltpu

LN_EPS = 1e-5
HEAD_DIM = 128
LANES = 128
SUBLANES_BF16 = 16
HALO_ROWS = 8
MXU_ROWS_PER_WEIGHT_TILE = 256
CONVERT_COLUMN_SPLITS = 4
VMEM_LIMIT_CAP = 60000 * 1024
COMPILER_SCRATCH_BYTES = 4 << 20

LOG2E = 1.4426950408889634

BF16 = jnp.bfloat16
F32 = jnp.float32


def _vmem_budget(*, pipelined, resident, temps):
    return 2 * pipelined + resident + temps + COMPILER_SCRATCH_BYTES


def _compiler_params(semantics, vmem_bytes):
    return pltpu.CompilerParams(
        dimension_semantics=semantics,
        vmem_limit_bytes=int(min(VMEM_LIMIT_CAP, vmem_bytes)),
    )


def _layer_norm(y, gain, bias):
    mu = jnp.mean(y, axis=-1, keepdims=True)
    yc = y - mu
    var = jnp.mean(yc * yc, axis=-1, keepdims=True)
    return yc * lax.rsqrt(var + LN_EPS) * gain + bias


def _dot(a, b):
    return jnp.dot(a, b, preferred_element_type=F32)


def _ffn_kernel(*refs, alpha, ns, last_is_pair, convert):
    x_ref, wg_a, wu_a, wo_a, wg_b, wu_b, wo_b, gb_ref = refs[:8]
    rest = refs[8:]
    if convert:
        (next_in, next_out), rest = rest[:2], rest[2:]
    (o_ref, ob_ref), rest = rest[:2], rest[2:]
    if convert:
        (next_in_bf, next_out_bf), rest = rest[:2], rest[2:]
    (xb_ref,) = rest
    s = pl.program_id(1)

    def convert_slab():
        if convert:
            next_in_bf[...] = next_in[...].astype(BF16)
            next_out_bf[...] = next_out[...].astype(BF16)

    tm = x_ref.shape[0]
    pair = [(wg_a, wu_a, wo_a), (wg_b, wu_b, wo_b)]

    def branch(chunks, weights):
        units = [(rows, w) for rows in chunks for w in weights]
        gates = [_dot(xb_ref[rows, :], wg[...]) for rows, (wg, _, _) in units]
        ups = [_dot(xb_ref[rows, :], wu[...]) for rows, (_, wu, _) in units]
        acts = [(g * jax.nn.sigmoid(g) * u).astype(BF16) for g, u in zip(gates, ups)]
        prods = [_dot(act, wo[...]) for act, (_, (_, _, wo)) in zip(acts, units)]
        n = len(weights)
        return [prods[c * n:(c + 1) * n] for c in range(len(chunks))]

    @pl.when(s == 0)
    def _():
        xb_ref[...] = x_ref[...].astype(BF16)
        p_a, p_b = branch([slice(0, tm)], pair)[0]
        convert_slab()
        o_ref[...] = (2.0 * alpha) * x_ref[...] + p_a
        o_ref[...] += p_b

    @pl.when(jnp.logical_and(s > 0, s < ns - 1))
    def _():
        p_a, p_b = branch([slice(0, tm)], pair)[0]
        convert_slab()
        o_ref[...] += p_a
        o_ref[...] += p_b

    @pl.when(s == ns - 1)
    def _():
        convert_slab()
        chunks = _row_chunks(tm)
        for rows, prods in zip(chunks, branch(chunks, pair if last_is_pair else pair[:1])):
            out = _layer_norm(0.5 * (o_ref[rows, :] + sum(prods)), gb_ref[0:1, :], gb_ref[1:2, :])
            o_ref[rows, :] = out
            ob_ref[rows, :] = out.astype(BF16)


def _ffn_ln(h, w_in, w_out, gain_bias, li, lk, *, alpha, tm, tf, next_f32=None):
    m, d = h.shape
    ff = w_out.shape[0]
    nf = ff // tf
    ns = pl.cdiv(nf, 2)
    nt = m // tm
    assert ns >= 2
    last_is_pair = nf % 2 == 0
    b_max = nf - 1 if last_is_pair else nf - 2

    def chunk_a(s):
        return 2 * s

    def chunk_b(s):
        return jnp.minimum(2 * s + 1, b_max)

    def weight_specs(chunk):
        return [
            pl.BlockSpec((d, tf), lambda i, s: (0, chunk(s))),
            pl.BlockSpec((d, tf), lambda i, s: (0, nf + chunk(s))),
            pl.BlockSpec((tf, d), lambda i, s: (chunk(s), 0)),
        ]

    in_specs = ([pl.BlockSpec((tm, d), lambda i, s: (i, 0))]
                + weight_specs(chunk_a) + weight_specs(chunk_b)
                + [pl.BlockSpec((None, None, 2, d), lambda i, s: (li, lk, 0, 0))])
    out_specs = [pl.BlockSpec((tm, d), lambda i, s: (i, 0))] * 2
    out_shape = [jax.ShapeDtypeStruct((m, d), F32), jax.ShapeDtypeStruct((m, d), BF16)]
    operands = [h, w_in, w_in, w_out, w_in, w_in, w_out, gain_bias]
    slab_bytes = 0
    if next_f32 is not None:
        next_in, next_out, ni, nj = next_f32
        for w_next in (next_in, next_out):
            rows, cols = w_next.shape[2] // nt, w_next.shape[3] // CONVERT_COLUMN_SPLITS
            assert rows * nt == w_next.shape[2] and rows % SUBLANES_BF16 == 0
            assert cols * CONVERT_COLUMN_SPLITS == w_next.shape[3] and cols % LANES == 0
            in_specs.append(pl.BlockSpec(
                (None, None, rows, cols),
                lambda i, s: (ni, nj, i, jnp.minimum(s, CONVERT_COLUMN_SPLITS - 1))))
            out_specs.append(pl.BlockSpec(
                (rows, cols), lambda i, s: (i, jnp.minimum(s, CONVERT_COLUMN_SPLITS - 1))))
            out_shape.append(jax.ShapeDtypeStruct(w_next.shape[2:], BF16))
            operands.append(w_next)
            slab_bytes += rows * cols * (4 + 2)
        assert ns >= CONVERT_COLUMN_SPLITS

    vmem = _vmem_budget(pipelined=2 * tm * d * 4 + tm * d * 2 + 6 * d * tf * 2 + 2 * d * 4
                        + slab_bytes, resident=tm * d * 2,
                        temps=4 * tm * tf * 4 + 2 * tm * tf * 2 + tm * d * 4
                        + 3 * MXU_ROWS_PER_WEIGHT_TILE * d * 4)
    return pl.pallas_call(
        functools.partial(_ffn_kernel, alpha=alpha, ns=ns, last_is_pair=last_is_pair,
                          convert=next_f32 is not None),
        grid=(nt, ns),
        in_specs=in_specs,
        out_specs=out_specs,
        out_shape=out_shape,
        scratch_shapes=[pltpu.VMEM((tm, d), BF16)],
        compiler_params=_compiler_params(("parallel", "arbitrary"), vmem),
        name="ffn_ln",
    )(*operands)


def _proj_kernel(xb_ref, w_ref, o_ref, *, out_scale):
    y = _dot(xb_ref[...], w_ref[...])
    if out_scale != 1.0:
        y = y * out_scale
    o_ref[...] = y.astype(o_ref.dtype)


def _proj(hb, w, li, *, tm, tn, out_scale=1.0):
    m, d = hb.shape
    n = w.shape[2]
    vmem = _vmem_budget(pipelined=tm * d * 2 + d * tn * 2 + tm * tn * 2,
                        resident=0, temps=2 * tm * tn * 4)
    return pl.pallas_call(
        functools.partial(_proj_kernel, out_scale=out_scale),
        grid=(m // tm, n // tn),
        in_specs=[
            pl.BlockSpec((tm, d), lambda i, j: (i, 0)),
            pl.BlockSpec((None, d, tn), lambda i, j: (li, 0, j)),
        ],
        out_specs=pl.BlockSpec((tm, tn), lambda i, j: (i, j)),
        out_shape=jax.ShapeDtypeStruct((m, n), BF16),
        compiler_params=_compiler_params(("parallel", "arbitrary"), vmem),
        name="proj",
    )(hb, w)


def _conv_in_kernel(xb_ref, wb_ref, wc_ref, wu_ref, b_ref, v_ref):
    xb = xb_ref[...]
    b_ref[...] = _dot(xb, wb_ref[...]).astype(BF16)
    v_ref[...] = (_dot(xb, wc_ref[...]) * _dot(xb, wu_ref[...])).astype(BF16)


def _conv_in(hb, w_in, li, *, tm, tn):
    m, d = hb.shape
    nd = d // tn
    vmem = _vmem_budget(pipelined=tm * d * 2 + 3 * d * tn * 2 + 2 * tm * tn * 2,
                        resident=0, temps=4 * tm * tn * 4)
    return pl.pallas_call(
        _conv_in_kernel,
        grid=(m // tm, nd),
        in_specs=[
            pl.BlockSpec((tm, d), lambda i, j: (i, 0)),
            pl.BlockSpec((None, d, tn), lambda i, j: (li, 0, j)),
            pl.BlockSpec((None, d, tn), lambda i, j: (li, 0, nd + j)),
            pl.BlockSpec((None, d, tn), lambda i, j: (li, 0, 2 * nd + j)),
        ],
        out_specs=[
            pl.BlockSpec((tm, tn), lambda i, j: (i, j)),
            pl.BlockSpec((tm, tn), lambda i, j: (i, j)),
        ],
        out_shape=[jax.ShapeDtypeStruct((m, d), BF16), jax.ShapeDtypeStruct((m, d), BF16)],
        compiler_params=_compiler_params(("parallel", "arbitrary"), vmem),
        name="conv_in",
    )(hb, w_in, w_in, w_in)


def _row_chunks(tm):
    rc = min(tm, MXU_ROWS_PER_WEIGHT_TILE)
    return [slice(r, r + rc) for r in range(0, tm, rc)]


def _residual_ln_chunks(chunks, products, x_ref, g_ref, b_ref, o_ref, alpha):
    for rows, prod in zip(chunks, products):
        o_ref[rows, :] = _layer_norm(alpha * x_ref[rows, :] + prod, g_ref[...], b_ref[...])


def _mix_out_kernel(lhs_ref, w_ref, x_ref, g_ref, b_ref, o_ref, *, alpha):
    chunks = _row_chunks(x_ref.shape[0])
    products = [_dot(lhs_ref[rows, :], w_ref[...]) for rows in chunks]
    _residual_ln_chunks(chunks, products, x_ref, g_ref, b_ref, o_ref, alpha)


def _conv_out_kernel(bg_ref, v_ref, halo_ref, cw_ref, cb_ref, w_ref, x_ref, g_ref, b_ref,
                     o_ref, *, alpha):
    chunks = _row_chunks(x_ref.shape[0])
    cw = cw_ref[...]
    products = []
    for rows in chunks:
        v0 = v_ref[rows, :].astype(F32)
        if rows.start == 0:
            before = halo_ref[...].astype(F32)
        else:
            before = v_ref[rows.start - SUBLANES_BF16:rows.start, :].astype(F32)
        prev1 = before[-1:]
        prev2 = before[-2:-1]
        row = lax.broadcasted_iota(jnp.int32, (v0.shape[0], 1), 0)
        v1 = jnp.where(row == 0, prev1, pltpu.roll(v0, 1, axis=0))
        v2 = jnp.where(row == 0, prev2, jnp.where(row == 1, prev1, pltpu.roll(v0, 2, axis=0)))
        conv = v2 * cw[0:1] + v1 * cw[1:2] + v0 * cw[2:3] + cb_ref[...]
        lhs = (bg_ref[rows, :].astype(F32) * conv).astype(BF16)
        products.append(_dot(lhs, w_ref[...]))
    _residual_ln_chunks(chunks, products, x_ref, g_ref, b_ref, o_ref, alpha)


def _ln_out(m, d, tm):
    return dict(out_specs=pl.BlockSpec((tm, d), lambda i: (i, 0)),
                out_shape=jax.ShapeDtypeStruct((m, d), F32))


def _mix_out_specs(tm, d, wi, li, lk):
    return [
        pl.BlockSpec((None, d, d), lambda i: (wi, 0, 0)),
        pl.BlockSpec((tm, d), lambda i: (i, 0)),
        pl.BlockSpec((None, None, 1, d), lambda i: (li, lk, 0, 0)),
        pl.BlockSpec((None, None, 1, d), lambda i: (li, lk, 0, 0)),
    ]


def _mix_out_ln(lhs, w, h, gain, bias, wi, li, lk, *, alpha, tm):
    m, d = h.shape
    vmem = _vmem_budget(pipelined=tm * d * 2 + d * d * 2 + 2 * tm * d * 4 + 2 * d * 4,
                        resident=0, temps=4 * tm * d * 4)
    return pl.pallas_call(
        functools.partial(_mix_out_kernel, alpha=alpha),
        grid=(m // tm,),
        in_specs=[pl.BlockSpec((tm, d), lambda i: (i, 0))] + _mix_out_specs(tm, d, wi, li, lk),
        **_ln_out(m, d, tm),
        compiler_params=_compiler_params(("parallel",), vmem),
        name="mix_out_ln",
    )(lhs, w, h, gain, bias)


def _conv_out_ln(bg, v, halo, cw, cb, w, h, gain, bias, ci, li, lk, *, alpha, tm):
    m, d = h.shape
    taps = cw.shape[1]
    vmem = _vmem_budget(pipelined=2 * tm * d * 2 + d * d * 2 + 2 * tm * d * 4
                        + (HALO_ROWS + taps + 3) * d * 4, resident=0, temps=8 * tm * d * 4)
    return pl.pallas_call(
        functools.partial(_conv_out_kernel, alpha=alpha),
        grid=(m // tm,),
        in_specs=[
            pl.BlockSpec((tm, d), lambda i: (i, 0)),
            pl.BlockSpec((tm, d), lambda i: (i, 0)),
            pl.BlockSpec((None, HALO_ROWS, d), lambda i: (i, 0, 0)),
            pl.BlockSpec((None, taps, d), lambda i: (ci, 0, 0)),
            pl.BlockSpec((None, 1, d), lambda i: (ci, 0, 0)),
        ] + _mix_out_specs(tm, d, ci, li, lk),
        **_ln_out(m, d, tm),
        compiler_params=_compiler_params(("parallel",), vmem),
        name="conv_out_ln",
    )(bg, v, halo, cw, cb, w, h, gain, bias)


def _suffix_sum_matrix(n):
    r = lax.broadcasted_iota(jnp.int32, (2 * n, n), 0)
    c = lax.broadcasted_iota(jnp.int32, (2 * n, n), 1)
    return (jnp.where(r >= n, r - n, r) > c).astype(BF16)


def _sb_raw(q_ref, lanes, k_blks):
    return [lax.dot_general(q_ref[:, ln], k, (((1,), (1,)), ((), ())), preferred_element_type=F32)
            for ln, k in zip(lanes, k_blks)]


def _sb_logs(zs, mask):
    sign_bit = jnp.uint32(1 << 31)
    log_betas, splits, totals = [], [], []
    for z in zs:
        neg_abs = lax.bitcast_convert_type(lax.bitcast_convert_type(z, jnp.uint32) | sign_bit, F32)
        soft = jnp.log(1.0 + jnp.exp2(neg_abs)) * LOG2E
        log_beta = jnp.minimum(z, 0.0) - soft
        log_keep = log_beta - z
        if mask is not None:
            log_keep = jnp.where(mask, log_keep, 0.0)
        hi = log_keep.astype(BF16)
        lo = (log_keep - hi.astype(F32)).astype(BF16)
        log_betas.append(log_beta)
        splits.append(jnp.concatenate([hi, lo], axis=1))
        totals.append(jnp.sum(log_keep, axis=-1, keepdims=True))
    return log_betas, splits, totals


def _sb_exp(log_beta, survive, carry, mask):
    a = jnp.exp2(log_beta + survive + carry)
    if mask is not None:
        a = jnp.where(mask, a, 0.0)
    return a.astype(BF16)


def _meta_suffix_sum_matrix(n_meta):
    r = lax.broadcasted_iota(jnp.int32, (2 * LANES, LANES), 0)
    c = lax.broadcasted_iota(jnp.int32, (2 * LANES, LANES), 1)
    r = jnp.where(r >= LANES, r - LANES, r)
    return jnp.logical_and(r // n_meta == c // n_meta, r > c).astype(BF16)


def _attn_kernel(q_ref, k_ref, v_ref, km_ref, vm_ref, o_ref, acc_ref, suffix_ref, raw_ref, w_ref,
                 *, tq, hp, n_meta):
    qi = pl.program_id(2)
    suffix_ref[...] = _suffix_sum_matrix(tq)
    row = lax.broadcasted_iota(jnp.int32, (tq, tq), 0)
    col = lax.broadcasted_iota(jnp.int32, (tq, tq), 1)
    causal = col < row
    lanes = [slice(g * HEAD_DIM, (g + 1) * HEAD_DIM) for g in range(hp)]

    def kv_blocks(ref, block):
        start = pl.multiple_of(block * tq, tq)
        return [ref[pl.ds(start, tq), ln] for ln in lanes]

    def accumulate_pv(block):
        for g, (ln, v) in enumerate(zip(lanes, kv_blocks(v_ref, block))):
            acc_ref[:, ln] += _dot(w_ref[g], v)

    def visit(mask, pv_block, next_block, carries):
        next_k = kv_blocks(k_ref, next_block)
        prev_v = None if pv_block is None else kv_blocks(v_ref, pv_block)
        totals, pending = [], None
        for g, ln in enumerate(lanes):
            (log_beta,), (split,), (total,) = _sb_logs([raw_ref[g]], mask)
            survive = _dot(split, suffix_ref[...])
            if prev_v is not None:
                acc_ref[:, ln] += _dot(w_ref[g], prev_v[g])
            raw_ref[g] = _sb_raw(q_ref, [ln], [next_k[g]])[0]
            if pending is not None:
                w_ref[g - 1] = _sb_exp(*pending, mask)
            pending = (log_beta, survive, carries[g])
            totals.append(total)
        w_ref[hp - 1] = _sb_exp(*pending, mask)
        return totals

    acc_ref[...] = jnp.zeros_like(acc_ref)
    for g, raw in enumerate(_sb_raw(q_ref, lanes, kv_blocks(k_ref, qi))):
        raw_ref[g] = raw
    carries = visit(causal, None, jnp.maximum(qi - 1, 0), [0.0] * hp)
    (meta_log_beta,), (meta_split,), _ = _sb_logs([_dot(q_ref[...], km_ref[...])], None)
    meta_survive = _dot(meta_split, _meta_suffix_sum_matrix(n_meta))

    def body(t, carries):
        block = qi - t
        totals = visit(None, block + 1, jnp.maximum(block - 1, 0), carries)
        return tuple(c + tot for c, tot in zip(carries, totals))

    carries = lax.fori_loop(1, qi + 1, body, tuple(carries))

    accumulate_pv(0)
    head_of_lane = lax.broadcasted_iota(jnp.int32, (tq, LANES), 1) // n_meta
    carry = carries[0]
    for g in range(1, hp):
        carry = jnp.where(head_of_lane == g, carries[g], carry)
    a = _sb_exp(meta_log_beta, meta_survive, carry, None)
    o_ref[...] = (acc_ref[...] + _dot(a, vm_ref[...])).astype(o_ref.dtype)


def _attention(q, kv, k_meta, v_meta, *, batch, seq, n_meta, tq, hp):
    m, d = q.shape
    width = hp * HEAD_DIM
    groups = d // width
    nq = seq // tq
    assert hp * n_meta == LANES
    vmem = _vmem_budget(pipelined=2 * (tq + seq + LANES) * width * 2,
                        resident=tq * width * 4 + 2 * tq * tq * 2 + hp * tq * tq * (4 + 2),
                        temps=hp * 16 * tq * tq * 4)
    return pl.pallas_call(
        functools.partial(_attn_kernel, tq=tq, hp=hp, n_meta=n_meta),
        grid=(batch, groups, nq),
        in_specs=[
            pl.BlockSpec((tq, width), lambda b, h, i: (b * nq + i, h)),
            pl.BlockSpec((seq, width), lambda b, h, i: (b, h)),
            pl.BlockSpec((seq, width), lambda b, h, i: (b, groups + h)),
            pl.BlockSpec((width, LANES), lambda b, h, i: (h, 0)),
            pl.BlockSpec((LANES, width), lambda b, h, i: (0, h)),
        ],
        out_specs=pl.BlockSpec((tq, width), lambda b, h, i: (b * nq + i, h)),
        out_shape=jax.ShapeDtypeStruct((m, d), BF16),
        scratch_shapes=[pltpu.VMEM((tq, width), F32), pltpu.VMEM((2 * tq, tq), BF16),
                        pltpu.VMEM((hp, tq, tq), F32), pltpu.VMEM((hp, tq, tq), BF16)],
        compiler_params=_compiler_params(("parallel", "parallel", "arbitrary"), vmem),
        name="sb_attention",
    )(q, kv, kv, k_meta, v_meta)


def _pack_meta_kv(kv_meta, *, hp):
    n_meta, two_d = kv_meta.shape
    d = two_d // 2
    head_in_group = (jnp.arange(d) // HEAD_DIM) % hp
    own = (head_in_group[:, None] == jnp.arange(hp)[None, :]).astype(kv_meta.dtype)
    k_meta = (kv_meta[:, :d].T[:, None, :] * own[:, :, None]).reshape(d, hp * n_meta)
    v_meta = (kv_meta[None, :, d:] * own.T[:, None, :]).reshape(hp * n_meta, d)
    return k_meta, v_meta


def _conv_halo(v_main, v_meta, *, batch, seq, tm):
    d = v_main.shape[1]
    nt = seq // tm
    tails = v_main.reshape(batch, nt, tm, d)[:, :nt - 1, tm - 2:, :]
    first = jnp.broadcast_to(v_meta[None, None, -2:, :], (batch, 1, 2, d))
    prev = jnp.concatenate([first, tails], axis=1).reshape(batch * nt, 2, d)
    return jnp.pad(prev, ((0, 0), (HALO_ROWS - 2, 0), (0, 0)))


def kernel(x, meta_tokens, ln_gain, ln_bias, ffn_w_in, ffn_w_out, conv_w_in, conv_w, conv_b,
           conv_w_out, sb_w_q, sb_w_kv, sb_w_o):
    batch, seq, d = x.shape
    n_meta = meta_tokens.shape[0]
    depth = ffn_w_in.shape[0]
    n_conv = conv_w_in.shape[0]
    alpha = float((2 * depth) ** 0.25)
    assert d % HEAD_DIM == 0 and n_meta % SUBLANES_BF16 == 0 and n_meta <= LANES

    tm = 512
    tf = 512
    tm_proj = 1024
    tn_proj = d
    tn_conv = 512
    tq = 256
    hp = 8

    cw_in = conv_w_in.astype(BF16)
    cw_out = conv_w_out.astype(BF16)
    w_q = sb_w_q.astype(BF16)
    w_kv = sb_w_kv.astype(BF16)[None]
    w_o = sb_w_o.astype(BF16)
    gain_bias = jnp.stack([ln_gain, ln_bias], axis=2)
    gain = ln_gain.reshape(depth, 3, 1, d)
    bias = ln_bias.reshape(depth, 3, 1, d)
    cb = conv_b.reshape(n_conv, 1, d)

    ffn_order = [(i, j) for i in range(depth) for j in range(2)]
    w_in = ffn_w_in[0, 0].astype(BF16)
    w_out = ffn_w_out[0, 0].astype(BF16)

    def ffn_pair(h, hm, w_in, w_out, i, j):
        k = ffn_order.index((i, j))
        lk = 2 * j
        if k + 1 < len(ffn_order):
            h, hb, next_w_in, next_w_out = _ffn_ln(
                h, w_in, w_out, gain_bias, i, lk, alpha=alpha, tm=tm, tf=tf,
                next_f32=(ffn_w_in, ffn_w_out) + ffn_order[k + 1])
        else:
            h, hb = _ffn_ln(h, w_in, w_out, gain_bias, i, lk, alpha=alpha, tm=tm, tf=tf)
            next_w_in = next_w_out = None
        hmb = None
        if i < n_conv:
            hm, hmb = _ffn_ln(hm, w_in, w_out, gain_bias, i, lk, alpha=alpha, tm=n_meta, tf=tf)
        return h, hb, hm, hmb, next_w_in, next_w_out

    h = x.reshape(batch * seq, d)
    hm = meta_tokens.astype(x.dtype)
    kv = k_meta = v_meta = None
    for i in range(depth):
        h, hb, hm, hmb, w_in, w_out = ffn_pair(h, hm, w_in, w_out, i, 0)
        if i < n_conv:
            bg, v = _conv_in(hb, cw_in, i, tm=tm_proj, tn=tn_conv)
            bgm, vm = _conv_in(hmb, cw_in, i, tm=n_meta, tn=tn_conv)
            halo = _conv_halo(v, vm, batch=batch, seq=seq, tm=tm)
            h = _conv_out_ln(bg, v, halo, conv_w, cb, cw_out, h, gain, bias, i, i, 1,
                             alpha=alpha, tm=tm)
            halo_m = jnp.zeros((1, HALO_ROWS, d), BF16)
            hm = _conv_out_ln(bgm, vm, halo_m, conv_w, cb, cw_out, hm, gain, bias, i, i, 1,
                              alpha=alpha, tm=n_meta)
        else:
            j = i - n_conv
            q = _proj(hb, w_q, j, tm=tm_proj, tn=tn_proj, out_scale=HEAD_DIM ** -0.5 * LOG2E)
            o = _attention(q, kv, k_meta, v_meta, batch=batch, seq=seq, n_meta=n_meta, tq=tq,
                           hp=hp)
            h = _mix_out_ln(o, w_o, h, gain, bias, j, i, 1, alpha=alpha, tm=tm)
        h, hb, hm, hmb, w_in, w_out = ffn_pair(h, hm, w_in, w_out, i, 1)
        if i == n_conv - 1:
            kv = _proj(hb, w_kv, 0, tm=tm_proj, tn=tn_proj)
            k_meta, v_meta = _pack_meta_kv(_proj(hmb, w_kv, 0, tm=n_meta, tn=tn_proj), hp=hp)
    return h.reshape(batch, seq, d)
```

```python
import functools

import jax
import jax.numpy as jnp
from jax import lax
from jax.experimental import pallas as pl
from jax.experimental.pallas import tpu as pltpu

LN_EPS = 1e-5
HEAD_DIM = 128
LANES = 128
SUBLANES_BF16 = 16
HALO_ROWS = 8
MXU_ROWS_PER_WEIGHT_TILE = 256
CONVERT_COLUMN_SPLITS = (4, 2)
VMEM_LIMIT_CAP = 60000 * 1024
COMPILER_SCRATCH_BYTES = 4 << 20

LOG2E = 1.4426950408889634

BF16 = jnp.bfloat16
F32 = jnp.float32


def _vmem_budget(*, pipelined, resident, temps):
    return 2 * pipelined + resident + temps + COMPILER_SCRATCH_BYTES


def _compiler_params(semantics, vmem_bytes):
    return pltpu.CompilerParams(
        dimension_semantics=semantics,
        vmem_limit_bytes=int(min(VMEM_LIMIT_CAP, vmem_bytes)),
    )


def _layer_norm(y, gain, bias):
    mu = jnp.mean(y, axis=-1, keepdims=True)
    yc = y - mu
    var = jnp.mean(yc * yc, axis=-1, keepdims=True)
    return yc * lax.rsqrt(var + LN_EPS) * gain + bias


def _dot(a, b):
    return jnp.dot(a, b, preferred_element_type=F32)


def _ffn_kernel(*refs, alpha, ns, last_is_pair, convert):
    x_ref, wg_a, wu_a, wo_a, wg_b, wu_b, wo_b, gb_ref = refs[:8]
    rest = refs[8:]
    if convert:
        (next_in, next_out), rest = rest[:2], rest[2:]
    (o_ref, ob_ref), rest = rest[:2], rest[2:]
    if convert:
        (next_in_bf, next_out_bf), rest = rest[:2], rest[2:]
    (xb_ref,) = rest
    s = pl.program_id(1)

    def convert_slab():
        if convert:
            next_in_bf[...] = next_in[...].astype(BF16)
            next_out_bf[...] = next_out[...].astype(BF16)

    tm = x_ref.shape[0]
    pair = [(wg_a, wu_a, wo_a), (wg_b, wu_b, wo_b)]

    def branch(chunks, weights):
        units = [(rows, w) for rows in chunks for w in weights]
        gates = [_dot(xb_ref[rows, :], wg[...]) for rows, (wg, _, _) in units]
        ups = [_dot(xb_ref[rows, :], wu[...]) for rows, (_, wu, _) in units]
        acts = [(g * jax.nn.sigmoid(g) * u).astype(BF16) for g, u in zip(gates, ups)]
        prods = [_dot(act, wo[...]) for act, (_, (_, _, wo)) in zip(acts, units)]
        n = len(weights)
        return [prods[c * n:(c + 1) * n] for c in range(len(chunks))]

    @pl.when(s == 0)
    def _():
        xb_ref[...] = x_ref[...].astype(BF16)
        p_a, p_b = branch([slice(0, tm)], pair)[0]
        convert_slab()
        o_ref[...] = (2.0 * alpha) * x_ref[...] + p_a
        o_ref[...] += p_b

    @pl.when(jnp.logical_and(s > 0, s < ns - 1))
    def _():
        p_a, p_b = branch([slice(0, tm)], pair)[0]
        convert_slab()
        o_ref[...] += p_a
        o_ref[...] += p_b

    @pl.when(s == ns - 1)
    def _():
        convert_slab()
        chunks = _row_chunks(tm)
        for rows, prods in zip(chunks, branch(chunks, pair if last_is_pair else pair[:1])):
            out = _layer_norm(0.5 * (o_ref[rows, :] + sum(prods)), gb_ref[0:1, :], gb_ref[1:2, :])
            o_ref[rows, :] = out
            ob_ref[rows, :] = out.astype(BF16)


def _ffn_ln(h, w_in, w_out, gain_bias, li, lk, *, alpha, tm, tf, next_f32=None):
    m, d = h.shape
    ff = w_out.shape[0]
    nf = ff // tf
    ns = pl.cdiv(nf, 2)
    nt = m // tm
    assert ns >= 2
    last_is_pair = nf % 2 == 0
    b_max = nf - 1 if last_is_pair else nf - 2

    def chunk_a(s):
        return 2 * s

    def chunk_b(s):
        return jnp.minimum(2 * s + 1, b_max)

    def weight_specs(chunk):
        return [
            pl.BlockSpec((d, tf), lambda i, s: (0, chunk(s))),
            pl.BlockSpec((d, tf), lambda i, s: (0, nf + chunk(s))),
            pl.BlockSpec((tf, d), lambda i, s: (chunk(s), 0)),
        ]

    in_specs = ([pl.BlockSpec((tm, d), lambda i, s: (i, 0))]
                + weight_specs(chunk_a) + weight_specs(chunk_b)
                + [pl.BlockSpec((None, None, 2, d), lambda i, s: (li, lk, 0, 0))])
    out_specs = [pl.BlockSpec((tm, d), lambda i, s: (i, 0))] * 2
    out_shape = [jax.ShapeDtypeStruct((m, d), F32), jax.ShapeDtypeStruct((m, d), BF16)]
    operands = [h, w_in, w_in, w_out, w_in, w_in, w_out, gain_bias]
    slab_bytes = 0
    if next_f32 is not None:
        next_in, next_out, ni, nj = next_f32
        for w_next, splits in zip((next_in, next_out), CONVERT_COLUMN_SPLITS):
            rows, cols = w_next.shape[2] // nt, w_next.shape[3] // splits
            assert rows * nt == w_next.shape[2] and rows % SUBLANES_BF16 == 0
            assert cols * splits == w_next.shape[3] and cols % LANES == 0 and ns >= splits

            def slab(i, s, splits=splits):
                return (i, jnp.minimum(s, splits - 1))

            in_specs.append(pl.BlockSpec((None, None, rows, cols),
                                         lambda i, s, slab=slab: (ni, nj) + slab(i, s)))
            out_specs.append(pl.BlockSpec((rows, cols), slab))
            out_shape.append(jax.ShapeDtypeStruct(w_next.shape[2:], BF16))
            operands.append(w_next)
            slab_bytes += rows * cols * (4 + 2)

    vmem = _vmem_budget(pipelined=2 * tm * d * 4 + tm * d * 2 + 6 * d * tf * 2 + 2 * d * 4
                        + slab_bytes, resident=tm * d * 2,
                        temps=4 * tm * tf * 4 + 2 * tm * tf * 2 + tm * d * 4
                        + 3 * MXU_ROWS_PER_WEIGHT_TILE * d * 4)
    return pl.pallas_call(
        functools.partial(_ffn_kernel, alpha=alpha, ns=ns, last_is_pair=last_is_pair,
                          convert=next_f32 is not None),
        grid=(nt, ns),
        in_specs=in_specs,
        out_specs=out_specs,
        out_shape=out_shape,
        scratch_shapes=[pltpu.VMEM((tm, d), BF16)],
        compiler_params=_compiler_params(("parallel", "arbitrary"), vmem),
        name="ffn_ln",
    )(*operands)


def _proj_kernel(xb_ref, w_ref, o_ref, *, out_scale):
    y = _dot(xb_ref[...], w_ref[...])
    if out_scale != 1.0:
        y = y * out_scale
    o_ref[...] = y.astype(o_ref.dtype)


def _proj(hb, w, li, *, tm, tn, out_scale=1.0):
    m, d = hb.shape
    n = w.shape[2]
    vmem = _vmem_budget(pipelined=tm * d * 2 + d * tn * 2 + tm * tn * 2,
                        resident=0, temps=2 * tm * tn * 4)
    return pl.pallas_call(
        functools.partial(_proj_kernel, out_scale=out_scale),
        grid=(m // tm, n // tn),
        in_specs=[
            pl.BlockSpec((tm, d), lambda i, j: (i, 0)),
            pl.BlockSpec((None, d, tn), lambda i, j: (li, 0, j)),
        ],
        out_specs=pl.BlockSpec((tm, tn), lambda i, j: (i, j)),
        out_shape=jax.ShapeDtypeStruct((m, n), BF16),
        compiler_params=_compiler_params(("parallel", "arbitrary"), vmem),
        name="proj",
    )(hb, w)


def _conv_in_kernel(xb_ref, wb_ref, wc_ref, wu_ref, b_ref, v_ref):
    xb = xb_ref[...]
    b_ref[...] = _dot(xb, wb_ref[...]).astype(BF16)
    v_ref[...] = (_dot(xb, wc_ref[...]) * _dot(xb, wu_ref[...])).astype(BF16)


def _conv_in(hb, w_in, li, *, tm, tn):
    m, d = hb.shape
    nd = d // tn
    vmem = _vmem_budget(pipelined=tm * d * 2 + 3 * d * tn * 2 + 2 * tm * tn * 2,
                        resident=0, temps=4 * tm * tn * 4)
    return pl.pallas_call(
        _conv_in_kernel,
        grid=(m // tm, nd),
        in_specs=[
            pl.BlockSpec((tm, d), lambda i, j: (i, 0)),
            pl.BlockSpec((None, d, tn), lambda i, j: (li, 0, j)),
            pl.BlockSpec((None, d, tn), lambda i, j: (li, 0, nd + j)),
            pl.BlockSpec((None, d, tn), lambda i, j: (li, 0, 2 * nd + j)),
        ],
        out_specs=[
            pl.BlockSpec((tm, tn), lambda i, j: (i, j)),
            pl.BlockSpec((tm, tn), lambda i, j: (i, j)),
        ],
        out_shape=[jax.ShapeDtypeStruct((m, d), BF16), jax.ShapeDtypeStruct((m, d), BF16)],
        compiler_params=_compiler_params(("parallel", "arbitrary"), vmem),
        name="conv_in",
    )(hb, w_in, w_in, w_in)


def _row_chunks(tm):
    rc = min(tm, MXU_ROWS_PER_WEIGHT_TILE)
    return [slice(r, r + rc) for r in range(0, tm, rc)]


def _residual_ln_chunks(chunks, products, x_ref, g_ref, b_ref, o_ref, alpha):
    for rows, prod in zip(chunks, products):
        o_ref[rows, :] = _layer_norm(alpha * x_ref[rows, :] + prod, g_ref[...], b_ref[...])


def _mix_out_kernel(lhs_ref, w_ref, x_ref, g_ref, b_ref, o_ref, *, alpha):
    chunks = _row_chunks(x_ref.shape[0])
    products = [_dot(lhs_ref[rows, :], w_ref[...]) for rows in chunks]
    _residual_ln_chunks(chunks, products, x_ref, g_ref, b_ref, o_ref, alpha)


def _conv_out_kernel(bg_ref, v_ref, halo_ref, cw_ref, cb_ref, w_ref, x_ref, g_ref, b_ref,
                     o_ref, *, alpha):
    chunks = _row_chunks(x_ref.shape[0])
    cw = cw_ref[...]
    products = []
    for rows in chunks:
        v0 = v_ref[rows, :].astype(F32)
        if rows.start == 0:
            before = halo_ref[...].astype(F32)
        else:
            before = v_ref[rows.start - SUBLANES_BF16:rows.start, :].astype(F32)
        prev1 = before[-1:]
        prev2 = before[-2:-1]
        row = lax.broadcasted_iota(jnp.int32, (v0.shape[0], 1), 0)
        v1 = jnp.where(row == 0, prev1, pltpu.roll(v0, 1, axis=0))
        v2 = jnp.where(row == 0, prev2, jnp.where(row == 1, prev1, pltpu.roll(v0, 2, axis=0)))
        conv = v2 * cw[0:1] + v1 * cw[1:2] + v0 * cw[2:3] + cb_ref[...]
        lhs = (bg_ref[rows, :].astype(F32) * conv).astype(BF16)
        products.append(_dot(lhs, w_ref[...]))
    _residual_ln_chunks(chunks, products, x_ref, g_ref, b_ref, o_ref, alpha)


def _ln_out(m, d, tm):
    return dict(out_specs=pl.BlockSpec((tm, d), lambda i: (i, 0)),
                out_shape=jax.ShapeDtypeStruct((m, d), F32))


def _mix_out_specs(tm, d, wi, li, lk):
    return [
        pl.BlockSpec((None, d, d), lambda i: (wi, 0, 0)),
        pl.BlockSpec((tm, d), lambda i: (i, 0)),
        pl.BlockSpec((None, None, 1, d), lambda i: (li, lk, 0, 0)),
        pl.BlockSpec((None, None, 1, d), lambda i: (li, lk, 0, 0)),
    ]


def _mix_out_ln(lhs, w, h, gain, bias, wi, li, lk, *, alpha, tm):
    m, d = h.shape
    vmem = _vmem_budget(pipelined=tm * d * 2 + d * d * 2 + 2 * tm * d * 4 + 2 * d * 4,
                        resident=0, temps=4 * tm * d * 4)
    return pl.pallas_call(
        functools.partial(_mix_out_kernel, alpha=alpha),
        grid=(m // tm,),
        in_specs=[pl.BlockSpec((tm, d), lambda i: (i, 0))] + _mix_out_specs(tm, d, wi, li, lk),
        **_ln_out(m, d, tm),
        compiler_params=_compiler_params(("parallel",), vmem),
        name="mix_out_ln",
    )(lhs, w, h, gain, bias)


def _conv_out_ln(bg, v, halo, cw, cb, w, h, gain, bias, ci, li, lk, *, alpha, tm):
    m, d = h.shape
    taps = cw.shape[1]
    vmem = _vmem_budget(pipelined=2 * tm * d * 2 + d * d * 2 + 2 * tm * d * 4
                        + (HALO_ROWS + taps + 3) * d * 4, resident=0, temps=8 * tm * d * 4)
    return pl.pallas_call(
        functools.partial(_conv_out_kernel, alpha=alpha),
        grid=(m // tm,),
        in_specs=[
            pl.BlockSpec((tm, d), lambda i: (i, 0)),
            pl.BlockSpec((tm, d), lambda i: (i, 0)),
            pl.BlockSpec((None, HALO_ROWS, d), lambda i: (i, 0, 0)),
            pl.BlockSpec((None, taps, d), lambda i: (ci, 0, 0)),
            pl.BlockSpec((None, 1, d), lambda i: (ci, 0, 0)),
        ] + _mix_out_specs(tm, d, ci, li, lk),
        **_ln_out(m, d, tm),
        compiler_params=_compiler_params(("parallel",), vmem),
        name="conv_out_ln",
    )(bg, v, halo, cw, cb, w, h, gain, bias)


def _suffix_sum_matrix(n):
    r = lax.broadcasted_iota(jnp.int32, (2 * n, n), 0)
    c = lax.broadcasted_iota(jnp.int32, (2 * n, n), 1)
    return (jnp.where(r >= n, r - n, r) > c).astype(BF16)


def _sb_raw(q_ref, lanes, k_blks):
    return [lax.dot_general(q_ref[:, ln], k, (((1,), (1,)), ((), ())), preferred_element_type=F32)
            for ln, k in zip(lanes, k_blks)]


def _sb_logs(zs, mask):
    sign_bit = jnp.uint32(1 << 31)
    log_betas, splits, totals = [], [], []
    for z in zs:
        neg_abs = lax.bitcast_convert_type(lax.bitcast_convert_type(z, jnp.uint32) | sign_bit, F32)
        soft = jnp.log(1.0 + jnp.exp2(neg_abs)) * LOG2E
        log_beta = jnp.minimum(z, 0.0) - soft
        log_keep = log_beta - z
        if mask is not None:
            log_keep = jnp.where(mask, log_keep, 0.0)
        hi = log_keep.astype(BF16)
        lo = (log_keep - hi.astype(F32)).astype(BF16)
        log_betas.append(log_beta)
        splits.append(jnp.concatenate([hi, lo], axis=1))
        totals.append(jnp.sum(log_keep, axis=-1, keepdims=True))
    return log_betas, splits, totals


def _sb_exp(log_beta, survive, carry, mask):
    a = jnp.exp2(log_beta + survive + carry)
    if mask is not None:
        a = jnp.where(mask, a, 0.0)
    return a.astype(BF16)


def _meta_suffix_sum_matrix(n_meta):
    r = lax.broadcasted_iota(jnp.int32, (2 * LANES, LANES), 0)
    c = lax.broadcasted_iota(jnp.int32, (2 * LANES, LANES), 1)
    r = jnp.where(r >= LANES, r - LANES, r)
    return jnp.logical_and(r // n_meta == c // n_meta, r > c).astype(BF16)


def _attn_kernel(q_ref, k_ref, v_ref, km_ref, vm_ref, o_ref, acc_ref, suffix_ref, raw_ref, w_ref,
                 *, tq, hp, n_meta):
    qi = pl.program_id(2)
    suffix_ref[...] = _suffix_sum_matrix(tq)
    row = lax.broadcasted_iota(jnp.int32, (tq, tq), 0)
    col = lax.broadcasted_iota(jnp.int32, (tq, tq), 1)
    causal = col < row
    lanes = [slice(g * HEAD_DIM, (g + 1) * HEAD_DIM) for g in range(hp)]

    def kv_blocks(ref, block):
        start = pl.multiple_of(block * tq, tq)
        return [ref[pl.ds(start, tq), ln] for ln in lanes]

    def accumulate_pv(block):
        for g, (ln, v) in enumerate(zip(lanes, kv_blocks(v_ref, block))):
            acc_ref[:, ln] += _dot(w_ref[g], v)

    def visit(mask, pv_block, next_block, carries):
        next_k = kv_blocks(k_ref, next_block)
        prev_v = None if pv_block is None else kv_blocks(v_ref, pv_block)
        totals, pending = [], None
        for g, ln in enumerate(lanes):
            (log_beta,), (split,), (total,) = _sb_logs([raw_ref[g]], mask)
            survive = _dot(split, suffix_ref[...])
            if prev_v is not None:
                acc_ref[:, ln] += _dot(w_ref[g], prev_v[g])
            raw_ref[g] = _sb_raw(q_ref, [ln], [next_k[g]])[0]
            if pending is not None:
                w_ref[g - 1] = _sb_exp(*pending, mask)
            pending = (log_beta, survive, carries[g])
            totals.append(total)
        w_ref[hp - 1] = _sb_exp(*pending, mask)
        return totals

    acc_ref[...] = jnp.zeros_like(acc_ref)
    for g, raw in enumerate(_sb_raw(q_ref, lanes, kv_blocks(k_ref, qi))):
        raw_ref[g] = raw
    carries = visit(causal, None, jnp.maximum(qi - 1, 0), [0.0] * hp)
    (meta_log_beta,), (meta_split,), _ = _sb_logs([_dot(q_ref[...], km_ref[...])], None)
    meta_survive = _dot(meta_split, _meta_suffix_sum_matrix(n_meta))

    def body(t, carries):
        block = qi - t
        totals = visit(None, block + 1, jnp.maximum(block - 1, 0), carries)
        return tuple(c + tot for c, tot in zip(carries, totals))

    carries = lax.fori_loop(1, qi + 1, body, tuple(carries))

    accumulate_pv(0)
    head_of_lane = lax.broadcasted_iota(jnp.int32, (tq, LANES), 1) // n_meta
    carry = carries[0]
    for g in range(1, hp):
        carry = jnp.where(head_of_lane == g, carries[g], carry)
    a = _sb_exp(meta_log_beta, meta_survive, carry, None)
    o_ref[...] = (acc_ref[...] + _dot(a, vm_ref[...])).astype(o_ref.dtype)


def _attention(q, kv, k_meta, v_meta, *, batch, seq, n_meta, tq, hp):
    m, d = q.shape
    width = hp * HEAD_DIM
    groups = d // width
    nq = seq // tq
    assert hp * n_meta == LANES
    vmem = _vmem_budget(pipelined=2 * (tq + seq + LANES) * width * 2,
                        resident=tq * width * 4 + 2 * tq * tq * 2 + hp * tq * tq * (4 + 2),
                        temps=hp * 16 * tq * tq * 4)
    return pl.pallas_call(
        functools.partial(_attn_kernel, tq=tq, hp=hp, n_meta=n_meta),
        grid=(batch, groups, nq),
        in_specs=[
            pl.BlockSpec((tq, width), lambda b, h, i: (b * nq + i, h)),
            pl.BlockSpec((seq, width), lambda b, h, i: (b, h)),
            pl.BlockSpec((seq, width), lambda b, h, i: (b, groups + h)),
            pl.BlockSpec((width, LANES), lambda b, h, i: (h, 0)),
            pl.BlockSpec((LANES, width), lambda b, h, i: (0, h)),
        ],
        out_specs=pl.BlockSpec((tq, width), lambda b, h, i: (b * nq + i, h)),
        out_shape=jax.ShapeDtypeStruct((m, d), BF16),
        scratch_shapes=[pltpu.VMEM((tq, width), F32), pltpu.VMEM((2 * tq, tq), BF16),
                        pltpu.VMEM((hp, tq, tq), F32), pltpu.VMEM((hp, tq, tq), BF16)],
        compiler_params=_compiler_params(("parallel", "parallel", "arbitrary"), vmem),
        name="sb_attention",
    )(q, kv, kv, k_meta, v_meta)


def _pack_meta_kv(kv_meta, *, hp):
    n_meta, two_d = kv_meta.shape
    d = two_d // 2
    head_in_group = (jnp.arange(d) // HEAD_DIM) % hp
    own = (head_in_group[:, None] == jnp.arange(hp)[None, :]).astype(kv_meta.dtype)
    k_meta = (kv_meta[:, :d].T[:, None, :] * own[:, :, None]).reshape(d, hp * n_meta)
    v_meta = (kv_meta[None, :, d:] * own.T[:, None, :]).reshape(hp * n_meta, d)
    return k_meta, v_meta


def _conv_halo(v_main, v_meta, *, batch, seq, tm):
    d = v_main.shape[1]
    nt = seq // tm
    tails = v_main.reshape(batch, nt, tm, d)[:, :nt - 1, tm - 2:, :]
    first = jnp.broadcast_to(v_meta[None, None, -2:, :], (batch, 1, 2, d))
    prev = jnp.concatenate([first, tails], axis=1).reshape(batch * nt, 2, d)
    return jnp.pad(prev, ((0, 0), (HALO_ROWS - 2, 0), (0, 0)))


def kernel(x, meta_tokens, ln_gain, ln_bias, ffn_w_in, ffn_w_out, conv_w_in, conv_w, conv_b,
           conv_w_out, sb_w_q, sb_w_kv, sb_w_o):
    batch, seq, d = x.shape
    n_meta = meta_tokens.shape[0]
    depth = ffn_w_in.shape[0]
    n_conv = conv_w_in.shape[0]
    alpha = float((2 * depth) ** 0.25)
    assert d % HEAD_DIM == 0 and n_meta % SUBLANES_BF16 == 0 and n_meta <= LANES

    tm = 512
    tf = 512
    tm_proj = 1024
    tn_proj = d
    tn_conv = 512
    tq = 256
    hp = 8

    cw_in = conv_w_in.astype(BF16)
    cw_out = conv_w_out.astype(BF16)
    w_q = sb_w_q.astype(BF16)
    w_kv = sb_w_kv.astype(BF16)[None]
    w_o = sb_w_o.astype(BF16)
    gain_bias = jnp.stack([ln_gain, ln_bias], axis=2)
    gain = ln_gain.reshape(depth, 3, 1, d)
    bias = ln_bias.reshape(depth, 3, 1, d)
    cb = conv_b.reshape(n_conv, 1, d)

    ffn_order = [(i, j) for i in range(depth) for j in range(2)]
    w_in = ffn_w_in[0, 0].astype(BF16)
    w_out = ffn_w_out[0, 0].astype(BF16)

    def ffn_pair(h, hm, w_in, w_out, i, j):
        k = ffn_order.index((i, j))
        lk = 2 * j
        if k + 1 < len(ffn_order):
            h, hb, next_w_in, next_w_out = _ffn_ln(
                h, w_in, w_out, gain_bias, i, lk, alpha=alpha, tm=tm, tf=tf,
                next_f32=(ffn_w_in, ffn_w_out) + ffn_order[k + 1])
        else:
            h, hb = _ffn_ln(h, w_in, w_out, gain_bias, i, lk, alpha=alpha, tm=tm, tf=tf)
            next_w_in = next_w_out = None
        hmb = None
        if i < n_conv:
            hm, hmb = _ffn_ln(hm, w_in, w_out, gain_bias, i, lk, alpha=alpha, tm=n_meta, tf=tf)
        return h, hb, hm, hmb, next_w_in, next_w_out

    h = x.reshape(batch * seq, d)
    hm = meta_tokens.astype(x.dtype)
    kv = k_meta = v_meta = None
    for i in range(depth):
        h, hb, hm, hmb, w_in, w_out = ffn_pair(h, hm, w_in, w_out, i, 0)
        if i < n_conv:
            bg, v = _conv_in(hb, cw_in, i, tm=tm_proj, tn=tn_conv)
            bgm, vm = _conv_in(hmb, cw_in, i, tm=n_meta, tn=tn_conv)
            halo = _conv_halo(v, vm, batch=batch, seq=seq, tm=tm)
            h = _conv_out_ln(bg, v, halo, conv_w, cb, cw_out, h, gain, bias, i, i, 1,
                             alpha=alpha, tm=tm)
            halo_m = jnp.zeros((1, HALO_ROWS, d), BF16)
            hm = _conv_out_ln(bgm, vm, halo_m, conv_w, cb, cw_out, hm, gain, bias, i, i, 1,
                              alpha=alpha, tm=n_meta)
        else:
            j = i - n_conv
            q = _proj(hb, w_q, j, tm=tm_proj, tn=tn_proj, out_scale=HEAD_DIM ** -0.5 * LOG2E)
            o = _attention(q, kv, k_meta, v_meta, batch=batch, seq=seq, n_meta=n_meta, tq=tq,
                           hp=hp)
            h = _mix_out_ln(o, w_o, h, gain, bias, j, i, 1, alpha=alpha, tm=tm)
        h, hb, hm, hmb, w_in, w_out = ffn_pair(h, hm, w_in, w_out, i, 1)
        if i == n_conv - 1:
            kv = _proj(hb, w_kv, 0, tm=tm_proj, tn=tn_proj)
            k_meta, v_meta = _pack_meta_kv(_proj(hmb, w_kv, 0, tm=n_meta, tn=tn_proj), hp=hp)
    return h.reshape(batch, seq, d)
```

```python
import functools

import jax
import jax.numpy as jnp
from jax import lax
from jax.experimental import pallas as pl
from jax.experimental.pallas import tpu as pltpu

LN_EPS = 1e-5
HEAD_DIM = 128
LANES = 128
SUBLANES_BF16 = 16
HALO_ROWS = 8
MXU_ROWS_PER_WEIGHT_TILE = 256
CONVERT_COLUMN_SPLITS = 4
VMEM_LIMIT_CAP = 60000 * 1024
COMPILER_SCRATCH_BYTES = 4 << 20

LOG2E = 1.4426950408889634
MASKED_LOGIT = -1e4

BF16 = jnp.bfloat16
F32 = jnp.float32


def _vmem_budget(*, pipelined, resident, temps):
    return 2 * pipelined + resident + temps + COMPILER_SCRATCH_BYTES


def _compiler_params(semantics, vmem_bytes):
    return pltpu.CompilerParams(
        dimension_semantics=semantics,
        vmem_limit_bytes=int(min(VMEM_LIMIT_CAP, vmem_bytes)),
    )


def _layer_norm(y, gain, bias):
    mu = jnp.mean(y, axis=-1, keepdims=True)
    yc = y - mu
    var = jnp.mean(yc * yc, axis=-1, keepdims=True)
    return yc * lax.rsqrt(var + LN_EPS) * gain + bias


def _dot(a, b):
    return jnp.dot(a, b, preferred_element_type=F32)


def _ffn_kernel(*refs, alpha, ns, last_is_pair, convert, emit_bf16):
    x_ref, wg_a, wu_a, wo_a, wg_b, wu_b, wo_b, gb_ref = refs[:8]
    rest = refs[8:]
    if convert:
        (next_in, next_out), rest = rest[:2], rest[2:]
    o_ref, rest = rest[0], rest[1:]
    if emit_bf16:
        ob_ref, rest = rest[0], rest[1:]
    if convert:
        (next_in_bf, next_out_bf), rest = rest[:2], rest[2:]
    (xb_ref,) = rest
    s = pl.program_id(1)

    def convert_slab():
        if convert:
            next_in_bf[...] = next_in[...].astype(BF16)
            next_out_bf[...] = next_out[...].astype(BF16)

    tm = x_ref.shape[0]
    pair = [(wg_a, wu_a, wo_a), (wg_b, wu_b, wo_b)]

    def branch(chunks, weights):
        units = [(rows, w) for rows in chunks for w in weights]
        gates = [_dot(xb_ref[rows, :], wg[...]) for rows, (wg, _, _) in units]
        ups = [_dot(xb_ref[rows, :], wu[...]) for rows, (_, wu, _) in units]
        acts = [(g * jax.nn.sigmoid(g) * u).astype(BF16) for g, u in zip(gates, ups)]
        prods = [_dot(act, wo[...]) for act, (_, (_, _, wo)) in zip(acts, units)]
        n = len(weights)
        return [prods[c * n:(c + 1) * n] for c in range(len(chunks))]

    @pl.when(s == 0)
    def _():
        xb_ref[...] = x_ref[...].astype(BF16)
        p_a, p_b = branch([slice(0, tm)], pair)[0]
        convert_slab()
        o_ref[...] = (2.0 * alpha) * x_ref[...] + p_a
        o_ref[...] += p_b

    @pl.when(jnp.logical_and(s > 0, s < ns - 1))
    def _():
        p_a, p_b = branch([slice(0, tm)], pair)[0]
        convert_slab()
        o_ref[...] += p_a
        o_ref[...] += p_b

    @pl.when(s == ns - 1)
    def _():
        convert_slab()
        chunks = _row_chunks(tm)
        for rows, prods in zip(chunks, branch(chunks, pair if last_is_pair else pair[:1])):
            out = _layer_norm(0.5 * (o_ref[rows, :] + sum(prods)), gb_ref[0:1, :], gb_ref[1:2, :])
            o_ref[rows, :] = out
            if emit_bf16:
                ob_ref[rows, :] = out.astype(BF16)


def _ffn_ln(h, w_in, w_out, gain_bias, li, lk, *, alpha, tm, tf, emit_bf16, next_f32=None):
    m, d = h.shape
    ff = w_out.shape[0]
    nf = ff // tf
    ns = pl.cdiv(nf, 2)
    nt = m // tm
    assert ns >= 2
    last_is_pair = nf % 2 == 0
    b_max = nf - 1 if last_is_pair else nf - 2

    def chunk_a(s):
        return 2 * s

    def chunk_b(s):
        return jnp.minimum(2 * s + 1, b_max)

    def weight_specs(chunk):
        return [
            pl.BlockSpec((d, tf), lambda i, s: (0, chunk(s))),
            pl.BlockSpec((d, tf), lambda i, s: (0, nf + chunk(s))),
            pl.BlockSpec((tf, d), lambda i, s: (chunk(s), 0)),
        ]

    in_specs = ([pl.BlockSpec((tm, d), lambda i, s: (i, 0))]
                + weight_specs(chunk_a) + weight_specs(chunk_b)
                + [pl.BlockSpec((None, None, 2, d), lambda i, s: (li, lk, 0, 0))])
    out_dtypes = [F32, BF16] if emit_bf16 else [F32]
    out_specs = [pl.BlockSpec((tm, d), lambda i, s: (i, 0)) for _ in out_dtypes]
    out_shape = [jax.ShapeDtypeStruct((m, d), dt) for dt in out_dtypes]
    operands = [h, w_in, w_in, w_out, w_in, w_in, w_out, gain_bias]
    slab_bytes = 0
    if next_f32 is not None:
        next_in, next_out, ni, nj = next_f32
        for w_next in (next_in, next_out):
            rows, cols = w_next.shape[2] // nt, w_next.shape[3] // CONVERT_COLUMN_SPLITS
            assert rows * nt == w_next.shape[2] and rows % SUBLANES_BF16 == 0
            assert cols * CONVERT_COLUMN_SPLITS == w_next.shape[3] and cols % LANES == 0
            in_specs.append(pl.BlockSpec(
                (None, None, rows, cols),
                lambda i, s: (ni, nj, i, jnp.minimum(s, CONVERT_COLUMN_SPLITS - 1))))
            out_specs.append(pl.BlockSpec(
                (rows, cols), lambda i, s: (i, jnp.minimum(s, CONVERT_COLUMN_SPLITS - 1))))
            out_shape.append(jax.ShapeDtypeStruct(w_next.shape[2:], BF16))
            operands.append(w_next)
            slab_bytes += rows * cols * (4 + 2)
        assert ns >= CONVERT_COLUMN_SPLITS

    vmem = _vmem_budget(pipelined=2 * tm * d * 4 + tm * d * 2 + 6 * d * tf * 2 + 2 * d * 4
                        + slab_bytes, resident=tm * d * 2,
                        temps=4 * tm * tf * 4 + 2 * tm * tf * 2 + tm * d * 4
                        + 3 * MXU_ROWS_PER_WEIGHT_TILE * d * 4)
    return pl.pallas_call(
        functools.partial(_ffn_kernel, alpha=alpha, ns=ns, last_is_pair=last_is_pair,
                          convert=next_f32 is not None, emit_bf16=emit_bf16),
        grid=(nt, ns),
        in_specs=in_specs,
        out_specs=out_specs,
        out_shape=out_shape,
        scratch_shapes=[pltpu.VMEM((tm, d), BF16)],
        compiler_params=_compiler_params(("parallel", "arbitrary"), vmem),
        name="ffn_ln",
    )(*operands)


def _proj_kernel(xb_ref, w_ref, o_ref, *, out_scale):
    y = _dot(xb_ref[...], w_ref[...])
    if out_scale != 1.0:
        y = y * out_scale
    o_ref[...] = y.astype(o_ref.dtype)


def _proj(hb, w, li, *, tm, tn, out_scale=1.0):
    m, d = hb.shape
    n = w.shape[2]
    vmem = _vmem_budget(pipelined=tm * d * 2 + d * tn * 2 + tm * tn * 2,
                        resident=0, temps=2 * tm * tn * 4)
    return pl.pallas_call(
        functools.partial(_proj_kernel, out_scale=out_scale),
        grid=(m // tm, n // tn),
        in_specs=[
            pl.BlockSpec((tm, d), lambda i, j: (i, 0)),
            pl.BlockSpec((None, d, tn), lambda i, j: (li, 0, j)),
        ],
        out_specs=pl.BlockSpec((tm, tn), lambda i, j: (i, j)),
        out_shape=jax.ShapeDtypeStruct((m, n), BF16),
        compiler_params=_compiler_params(("parallel", "arbitrary"), vmem),
        name="proj",
    )(hb, w)


def _conv_in_kernel(xb_ref, wb_ref, wc_ref, wu_ref, b_ref, v_ref):
    xb = xb_ref[...]
    b_ref[...] = _dot(xb, wb_ref[...]).astype(BF16)
    v_ref[...] = (_dot(xb, wc_ref[...]) * _dot(xb, wu_ref[...])).astype(BF16)


def _conv_in(hb, w_in, li, *, tm, tn):
    m, d = hb.shape
    nd = d // tn
    vmem = _vmem_budget(pipelined=tm * d * 2 + 3 * d * tn * 2 + 2 * tm * tn * 2,
                        resident=0, temps=4 * tm * tn * 4)
    return pl.pallas_call(
        _conv_in_kernel,
        grid=(m // tm, nd),
        in_specs=[
            pl.BlockSpec((tm, d), lambda i, j: (i, 0)),
            pl.BlockSpec((None, d, tn), lambda i, j: (li, 0, j)),
            pl.BlockSpec((None, d, tn), lambda i, j: (li, 0, nd + j)),
            pl.BlockSpec((None, d, tn), lambda i, j: (li, 0, 2 * nd + j)),
        ],
        out_specs=[
            pl.BlockSpec((tm, tn), lambda i, j: (i, j)),
            pl.BlockSpec((tm, tn), lambda i, j: (i, j)),
        ],
        out_shape=[jax.ShapeDtypeStruct((m, d), BF16), jax.ShapeDtypeStruct((m, d), BF16)],
        compiler_params=_compiler_params(("parallel", "arbitrary"), vmem),
        name="conv_in",
    )(hb, w_in, w_in, w_in)


def _row_chunks(tm):
    rc = min(tm, MXU_ROWS_PER_WEIGHT_TILE)
    return [slice(r, r + rc) for r in range(0, tm, rc)]


def _residual_ln_chunks(chunks, products, x_ref, g_ref, b_ref, o_ref, alpha):
    for rows, prod in zip(chunks, products):
        o_ref[rows, :] = _layer_norm(alpha * x_ref[rows, :] + prod, g_ref[...], b_ref[...])


def _mix_out_kernel(lhs_ref, w_ref, x_ref, g_ref, b_ref, o_ref, *, alpha):
    chunks = _row_chunks(x_ref.shape[0])
    products = [_dot(lhs_ref[rows, :], w_ref[...]) for rows in chunks]
    _residual_ln_chunks(chunks, products, x_ref, g_ref, b_ref, o_ref, alpha)


def _conv_out_kernel(bg_ref, v_ref, halo_ref, cw_ref, cb_ref, w_ref, x_ref, g_ref, b_ref,
                     o_ref, *, alpha):
    chunks = _row_chunks(x_ref.shape[0])
    cw = cw_ref[...]
    products = []
    for rows in chunks:
        v0 = v_ref[rows, :].astype(F32)
        if rows.start == 0:
            before = halo_ref[...].astype(F32)
        else:
            before = v_ref[rows.start - SUBLANES_BF16:rows.start, :].astype(F32)
        prev1 = before[-1:]
        prev2 = before[-2:-1]
        row = lax.broadcasted_iota(jnp.int32, (v0.shape[0], 1), 0)
        v1 = jnp.where(row == 0, prev1, pltpu.roll(v0, 1, axis=0))
        v2 = jnp.where(row == 0, prev2, jnp.where(row == 1, prev1, pltpu.roll(v0, 2, axis=0)))
        conv = v2 * cw[0:1] + v1 * cw[1:2] + v0 * cw[2:3] + cb_ref[...]
        lhs = (bg_ref[rows, :].astype(F32) * conv).astype(BF16)
        products.append(_dot(lhs, w_ref[...]))
    _residual_ln_chunks(chunks, products, x_ref, g_ref, b_ref, o_ref, alpha)


def _ln_out(m, d, tm):
    return dict(out_specs=pl.BlockSpec((tm, d), lambda i: (i, 0)),
                out_shape=jax.ShapeDtypeStruct((m, d), F32))


def _mix_out_specs(tm, d, wi, li, lk):
    return [
        pl.BlockSpec((None, d, d), lambda i: (wi, 0, 0)),
        pl.BlockSpec((tm, d), lambda i: (i, 0)),
        pl.BlockSpec((None, None, 1, d), lambda i: (li, lk, 0, 0)),
        pl.BlockSpec((None, None, 1, d), lambda i: (li, lk, 0, 0)),
    ]


def _mix_out_ln(lhs, w, h, gain, bias, wi, li, lk, *, alpha, tm):
    m, d = h.shape
    vmem = _vmem_budget(pipelined=tm * d * 2 + d * d * 2 + 2 * tm * d * 4 + 2 * d * 4,
                        resident=0, temps=4 * tm * d * 4)
    return pl.pallas_call(
        functools.partial(_mix_out_kernel, alpha=alpha),
        grid=(m // tm,),
        in_specs=[pl.BlockSpec((tm, d), lambda i: (i, 0))] + _mix_out_specs(tm, d, wi, li, lk),
        **_ln_out(m, d, tm),
        compiler_params=_compiler_params(("parallel",), vmem),
        name="mix_out_ln",
    )(lhs, w, h, gain, bias)


def _conv_out_ln(bg, v, halo, cw, cb, w, h, gain, bias, ci, li, lk, *, alpha, tm):
    m, d = h.shape
    taps = cw.shape[1]
    vmem = _vmem_budget(pipelined=2 * tm * d * 2 + d * d * 2 + 2 * tm * d * 4
                        + (HALO_ROWS + taps + 3) * d * 4, resident=0, temps=8 * tm * d * 4)
    return pl.pallas_call(
        functools.partial(_conv_out_kernel, alpha=alpha),
        grid=(m // tm,),
        in_specs=[
            pl.BlockSpec((tm, d), lambda i: (i, 0)),
            pl.BlockSpec((tm, d), lambda i: (i, 0)),
            pl.BlockSpec((None, HALO_ROWS, d), lambda i: (i, 0, 0)),
            pl.BlockSpec((None, taps, d), lambda i: (ci, 0, 0)),
            pl.BlockSpec((None, 1, d), lambda i: (ci, 0, 0)),
        ] + _mix_out_specs(tm, d, ci, li, lk),
        **_ln_out(m, d, tm),
        compiler_params=_compiler_params(("parallel",), vmem),
        name="conv_out_ln",
    )(bg, v, halo, cw, cb, w, h, gain, bias)


def _suffix_sum_matrix(n):
    r = lax.broadcasted_iota(jnp.int32, (2 * n, n), 0)
    c = lax.broadcasted_iota(jnp.int32, (2 * n, n), 1)
    return (jnp.where(r >= n, r - n, r) > c).astype(BF16)


def _sb_raw(q_ref, lanes, k_blks):
    return [lax.dot_general(q_ref[:, ln], k, (((1,), (1,)), ((), ())), preferred_element_type=F32)
            for ln, k in zip(lanes, k_blks)]


def _sb_logs(zs, mask):
    sign_bit = jnp.uint32(1 << 31)
    log_betas, splits, totals = [], [], []
    for z in zs:
        if mask is not None:
            z = jnp.where(mask, z, MASKED_LOGIT)
        neg_abs = lax.bitcast_convert_type(lax.bitcast_convert_type(z, jnp.uint32) | sign_bit, F32)
        soft = jnp.log(1.0 + jnp.exp2(neg_abs)) * LOG2E
        log_beta = jnp.minimum(z, 0.0) - soft
        log_keep = log_beta - z
        hi = log_keep.astype(BF16)
        lo = (log_keep - hi.astype(F32)).astype(BF16)
        log_betas.append(log_beta)
        splits.append(jnp.concatenate([hi, lo], axis=1))
        totals.append(jnp.sum(log_keep, axis=-1, keepdims=True))
    return log_betas, splits, totals


def _sb_exp(log_beta, survive, carry):
    return jnp.exp2(log_beta + survive + carry).astype(BF16)


def _meta_suffix_sum_matrix(n_meta):
    r = lax.broadcasted_iota(jnp.int32, (2 * LANES, LANES), 0)
    c = lax.broadcasted_iota(jnp.int32, (2 * LANES, LANES), 1)
    r = jnp.where(r >= LANES, r - LANES, r)
    return jnp.logical_and(r // n_meta == c // n_meta, r > c).astype(BF16)


def _attn_kernel(q_ref, k_ref, v_ref, km_ref, vm_ref, o_ref, acc_ref, suffix_ref, raw_ref, w_ref,
                 *, tq, hp, n_meta):
    qi = pl.program_id(2)
    suffix_ref[...] = _suffix_sum_matrix(tq)
    row = lax.broadcasted_iota(jnp.int32, (tq, tq), 0)
    col = lax.broadcasted_iota(jnp.int32, (tq, tq), 1)
    causal = col < row
    lanes = [slice(g * HEAD_DIM, (g + 1) * HEAD_DIM) for g in range(hp)]

    def kv_blocks(ref, block):
        start = pl.multiple_of(block * tq, tq)
        return [ref[pl.ds(start, tq), ln] for ln in lanes]

    def accumulate_pv(block):
        for g, (ln, v) in enumerate(zip(lanes, kv_blocks(v_ref, block))):
            acc_ref[:, ln] += _dot(w_ref[g], v)

    def visit(mask, pv_block, next_block, carries):
        next_k = kv_blocks(k_ref, next_block)
        prev_v = None if pv_block is None else kv_blocks(v_ref, pv_block)
        totals, pending = [], None
        for g, ln in enumerate(lanes):
            (log_beta,), (split,), (total,) = _sb_logs([raw_ref[g]], mask)
            survive = _dot(split, suffix_ref[...])
            if prev_v is not None:
                acc_ref[:, ln] += _dot(w_ref[g], prev_v[g])
            raw_ref[g] = _sb_raw(q_ref, [ln], [next_k[g]])[0]
            if pending is not None:
                w_ref[g - 1] = _sb_exp(*pending)
            pending = (log_beta, survive, carries[g])
            totals.append(total)
        w_ref[hp - 1] = _sb_exp(*pending)
        return totals

    acc_ref[...] = jnp.zeros_like(acc_ref)
    for g, raw in enumerate(_sb_raw(q_ref, lanes, kv_blocks(k_ref, qi))):
        raw_ref[g] = raw
    carries = visit(causal, None, jnp.maximum(qi - 1, 0), [0.0] * hp)
    (meta_log_beta,), (meta_split,), _ = _sb_logs([_dot(q_ref[...], km_ref[...])], None)
    meta_survive = _dot(meta_split, _meta_suffix_sum_matrix(n_meta))

    def body(t, carries):
        block = qi - t
        totals = visit(None, block + 1, jnp.maximum(block - 1, 0), carries)
        return tuple(c + tot for c, tot in zip(carries, totals))

    carries = lax.fori_loop(1, qi + 1, body, tuple(carries))

    accumulate_pv(0)
    head_of_lane = lax.broadcasted_iota(jnp.int32, (tq, LANES), 1) // n_meta
    carry = carries[0]
    for g in range(1, hp):
        carry = jnp.where(head_of_lane == g, carries[g], carry)
    a = _sb_exp(meta_log_beta, meta_survive, carry)
    o_ref[...] = (acc_ref[...] + _dot(a, vm_ref[...])).astype(o_ref.dtype)


def _attention(q, kv, k_meta, v_meta, *, batch, seq, n_meta, tq, hp):
    m, d = q.shape
    width = hp * HEAD_DIM
    groups = d // width
    nq = seq // tq
    assert hp * n_meta == LANES
    vmem = _vmem_budget(pipelined=2 * (tq + seq + LANES) * width * 2,
                        resident=tq * width * 4 + 2 * tq * tq * 2 + hp * tq * tq * (4 + 2),
                        temps=hp * 16 * tq * tq * 4)
    return pl.pallas_call(
        functools.partial(_attn_kernel, tq=tq, hp=hp, n_meta=n_meta),
        grid=(batch, groups, nq),
        in_specs=[
            pl.BlockSpec((tq, width), lambda b, h, i: (b * nq + i, h)),
            pl.BlockSpec((seq, width), lambda b, h, i: (b, h)),
            pl.BlockSpec((seq, width), lambda b, h, i: (b, groups + h)),
            pl.BlockSpec((width, LANES), lambda b, h, i: (h, 0)),
            pl.BlockSpec((LANES, width), lambda b, h, i: (0, h)),
        ],
        out_specs=pl.BlockSpec((tq, width), lambda b, h, i: (b * nq + i, h)),
        out_shape=jax.ShapeDtypeStruct((m, d), BF16),
        scratch_shapes=[pltpu.VMEM((tq, width), F32), pltpu.VMEM((2 * tq, tq), BF16),
                        pltpu.VMEM((hp, tq, tq), F32), pltpu.VMEM((hp, tq, tq), BF16)],
        compiler_params=_compiler_params(("parallel", "parallel", "arbitrary"), vmem),
        name="sb_attention",
    )(q, kv, kv, k_meta, v_meta)


def _pack_meta_kv(kv_meta, *, hp):
    n_meta, two_d = kv_meta.shape
    d = two_d // 2
    head_in_group = (jnp.arange(d) // HEAD_DIM) % hp
    own = (head_in_group[:, None] == jnp.arange(hp)[None, :]).astype(kv_meta.dtype)
    k_meta = (kv_meta[:, :d].T[:, None, :] * own[:, :, None]).reshape(d, hp * n_meta)
    v_meta = (kv_meta[None, :, d:] * own.T[:, None, :]).reshape(hp * n_meta, d)
    return k_meta, v_meta


def _conv_halo(v_main, v_meta, *, batch, seq, tm):
    d = v_main.shape[1]
    nt = seq // tm
    tails = v_main.reshape(batch, nt, tm, d)[:, :nt - 1, tm - 2:, :]
    first = jnp.broadcast_to(v_meta[None, None, -2:, :], (batch, 1, 2, d))
    prev = jnp.concatenate([first, tails], axis=1).reshape(batch * nt, 2, d)
    return jnp.pad(prev, ((0, 0), (HALO_ROWS - 2, 0), (0, 0)))


def kernel(x, meta_tokens, ln_gain, ln_bias, ffn_w_in, ffn_w_out, conv_w_in, conv_w, conv_b,
           conv_w_out, sb_w_q, sb_w_kv, sb_w_o):
    batch, seq, d = x.shape
    n_meta = meta_tokens.shape[0]
    depth = ffn_w_in.shape[0]
    n_conv = conv_w_in.shape[0]
    alpha = float((2 * depth) ** 0.25)
    assert d % HEAD_DIM == 0 and n_meta % SUBLANES_BF16 == 0 and n_meta <= LANES

    tm = 512
    tf = 512
    tm_proj = 1024
    tn_proj = d
    tn_conv = 512
    tq = 256
    hp = 8

    cw_in = conv_w_in.astype(BF16)
    cw_out = conv_w_out.astype(BF16)
    w_q = sb_w_q.astype(BF16)
    w_kv = sb_w_kv.astype(BF16)[None]
    w_o = sb_w_o.astype(BF16)
    gain_bias = jnp.stack([ln_gain, ln_bias], axis=2)
    gain = ln_gain.reshape(depth, 3, 1, d)
    bias = ln_bias.reshape(depth, 3, 1, d)
    cb = conv_b.reshape(n_conv, 1, d)

    ffn_order = [(i, j) for i in range(depth) for j in range(2)]
    w_in = ffn_w_in[0, 0].astype(BF16)
    w_out = ffn_w_out[0, 0].astype(BF16)

    def ffn_pair(h, hm, w_in, w_out, i, j):
        k = ffn_order.index((i, j))
        lk = 2 * j
        emit = j == 0 or i == n_conv - 1
        next_f32 = (ffn_w_in, ffn_w_out) + ffn_order[k + 1] if k + 1 < len(ffn_order) else None
        outs = _ffn_ln(h, w_in, w_out, gain_bias, i, lk, alpha=alpha, tm=tm, tf=tf,
                       emit_bf16=emit, next_f32=next_f32)
        h, hb = (outs[0], outs[1]) if emit else (outs[0], None)
        next_w_in, next_w_out = outs[-2:] if next_f32 is not None else (None, None)
        hmb = None
        if i < n_conv:
            outs = _ffn_ln(hm, w_in, w_out, gain_bias, i, lk, alpha=alpha, tm=n_meta, tf=tf,
                           emit_bf16=emit)
            hm, hmb = (outs[0], outs[1]) if emit else (outs[0], None)
        return h, hb, hm, hmb, next_w_in, next_w_out

    h = x.reshape(batch * seq, d)
    hm = meta_tokens.astype(x.dtype)
    kv = k_meta = v_meta = None
    for i in range(depth):
        h, hb, hm, hmb, w_in, w_out = ffn_pair(h, hm, w_in, w_out, i, 0)
        if i < n_conv:
            bg, v = _conv_in(hb, cw_in, i, tm=tm_proj, tn=tn_conv)
            bgm, vm = _conv_in(hmb, cw_in, i, tm=n_meta, tn=tn_conv)
            halo = _conv_halo(v, vm, batch=batch, seq=seq, tm=tm)
            h = _conv_out_ln(bg, v, halo, conv_w, cb, cw_out, h, gain, bias, i, i, 1,
                             alpha=alpha, tm=tm)
            halo_m = jnp.zeros((1, HALO_ROWS, d), BF16)
            hm = _conv_out_ln(bgm, vm, halo_m, conv_w, cb, cw_out, hm, gain, bias, i, i, 1,
                              alpha=alpha, tm=n_meta)
        else:
            j = i - n_conv
            q = _proj(hb, w_q, j, tm=tm_proj, tn=tn_proj, out_scale=HEAD_DIM ** -0.5 * LOG2E)
            o = _attention(q, kv, k_meta, v_meta, batch=batch, seq=seq, n_meta=n_meta, tq=tq,
                           hp=hp)
            h = _mix_out_ln(o, w_o, h, gain, bias, j, i, 1, alpha=alpha, tm=tm)
        h, hb, hm, hmb, w_in, w_out = ffn_pair(h, hm, w_in, w_out, i, 1)
        if i == n_conv - 1:
            kv = _proj(hb, w_kv, 0, tm=tm_proj, tn=tn_proj)
            k_meta, v_meta = _pack_meta_kv(_proj(hmb, w_kv, 0, tm=n_meta, tn=tn_proj), hp=hp)
    return h.reshape(batch, seq, d)
```
